```python
import math
import jax, jax.numpy as jnp
from jax import lax
import numpy as np

D_MODEL = 2048
BATCH = 2
SEQ = 4096
DEPTH = 2

N_MIXERS = 2
DA_HEADS = 8
DA_HEAD_DIM = D_MODEL // DA_HEADS // 2
DA_V_DIM = 2 * DA_HEAD_DIM
SB_HEADS = 16
SB_HEAD_DIM = D_MODEL // SB_HEADS
N_EXPERTS = 32
TOP_K = 4
D_FF = D_MODEL
SWIGLU_ALPHA = 1.702
SWIGLU_LIMIT = 7.0
ROUTER_BIAS_SCALE = 0.01
ROPE_THETA = 10000.0
Q_BLOCK = 128
MOE_BLOCK = 128
LN_EPS = 1e-5
DEEPNORM_ALPHA = (2 * DEPTH) ** 0.25
DEEPNORM_BETA = (8 * DEPTH) ** -0.25
N_DA_LAYERS = (DEPTH + 1) // 2
N_SB_LAYERS = DEPTH // 2

kernel_name = "diffattn_stickbreak_moe_deepnorm_hybrid"


def layer_norm(x, g, b):
    xf = x.astype(jnp.float32)
    mu = jnp.mean(xf, -1, keepdims=True)
    var = jnp.mean(jnp.square(xf - mu), -1, keepdims=True)
    return ((xf - mu) * lax.rsqrt(var + LN_EPS) * g.astype(jnp.float32) + b.astype(jnp.float32)).astype(x.dtype)


def rope_cos_sin(positions, dim):
    inv_freq = ROPE_THETA ** (-jnp.arange(0, dim, 2, dtype=jnp.float32) / dim)
    ang = positions.astype(jnp.float32)[..., None] * inv_freq
    return jnp.cos(ang), jnp.sin(ang)


def apply_rope(x, cos, sin):
    x1, x2 = jnp.split(x, 2, axis=-1)
    c = cos[:, :, None, :]
    s = sin[:, :, None, :]
    return jnp.concatenate([x1 * c - x2 * s, x2 * c + x1 * s], axis=-1).astype(x.dtype)


def diff_attention(x, cos, sin, w_in, lam_q1, lam_k1, lam_q2, lam_k2, subln_g, w_out, lambda_init):
    B, S, _ = x.shape
    H, d = DA_HEADS, DA_HEAD_DIM
    NB = S // Q_BLOCK
    proj = x @ w_in
    q, k, v = jnp.split(proj, [2 * H * d, 4 * H * d], axis=-1)
    q = apply_rope(q.reshape(B, S, 2 * H, d), cos, sin)
    k = apply_rope(k.reshape(B, S, 2 * H, d), cos, sin)
    v = v.reshape(B, S, H, DA_V_DIM)
    f32 = jnp.float32
    lam = (jnp.exp(jnp.sum(lam_q1.astype(f32) * lam_k1.astype(f32)))
           - jnp.exp(jnp.sum(lam_q2.astype(f32) * lam_k2.astype(f32))) + lambda_init)
    scale = d ** -0.5
    qb = q.reshape(B, NB, Q_BLOCK, 2 * H, d).transpose(1, 0, 3, 2, 4)
    kt = k.transpose(0, 2, 1, 3)
    vt = v.transpose(0, 2, 1, 3)
    key_pos = jnp.arange(S)

    def block(args):
        q_blk, start = args
        s = jnp.einsum('bhqd,bhkd->bhqk', q_blk, kt).astype(f32) * scale
        q_pos = start + jnp.arange(Q_BLOCK)
        causal = key_pos[None, :] <= q_pos[:, None]
        s = jnp.where(causal, s, -jnp.inf)
        p = jax.nn.softmax(s, axis=-1).reshape(B, H, 2, Q_BLOCK, S)
        a = p[:, :, 0] - lam * p[:, :, 1]
        return jnp.einsum('bhqk,bhkv->bhqv', a.astype(vt.dtype), vt)

    o = lax.map(block, (qb, jnp.arange(NB) * Q_BLOCK))
    o = o.transpose(1, 0, 3, 2, 4).reshape(B, S, H, DA_V_DIM)
    of = o.astype(f32)
    of = of * lax.rsqrt(jnp.mean(jnp.square(of), -1, keepdims=True) + LN_EPS) * subln_g.astype(f32)
    of = of * (1.0 - lambda_init)
    return of.astype(x.dtype).reshape(B, S, H * DA_V_DIM) @ w_out


def stick_breaking_attention(x, w_in, w_out):
    B, S, _ = x.shape
    H, d = SB_HEADS, SB_HEAD_DIM
    NB = S // Q_BLOCK
    q, k, v = jnp.split(x @ w_in, 3, axis=-1)
    qb = q.reshape(B, NB, Q_BLOCK, H, d).transpose(1, 0, 3, 2, 4)
    kt = k.reshape(B, S, H, d).transpose(0, 2, 1, 3)
    vt = v.reshape(B, S, H, d).transpose(0, 2, 1, 3)
    key_pos = jnp.arange(S)
    scale = d ** -0.5

    def block(args):
        q_blk, start = args
        z = jnp.einsum('bhqd,bhkd->bhqk', q_blk, kt).astype(jnp.float32) * scale
        q_pos = start + jnp.arange(Q_BLOCK)
        strict = key_pos[None, :] < q_pos[:, None]
        log_beta = jax.nn.log_sigmoid(z)
        log_1m_beta = jnp.where(strict, jax.nn.log_sigmoid(-z), 0.0)
        after = lax.cumsum(log_1m_beta, axis=3, reverse=True) - log_1m_beta
        a = jnp.where(strict, jnp.exp(log_beta + after), 0.0)
        return jnp.einsum('bhqk,bhkd->bhqd', a.astype(vt.dtype), vt)

    o = lax.map(block, (qb, jnp.arange(NB) * Q_BLOCK))
    o = o.transpose(1, 0, 3, 2, 4).reshape(B, S, H * d)
    return o @ w_out


def moe_ffn(x, w_router, b_router, w1, b1, w2, b2):
    B, S, D = x.shape
    T = B * S
    E, K, MB = N_EXPERTS, TOP_K, MOE_BLOCK
    xf = x.reshape(T, D)
    logits = (xf @ w_router).astype(jnp.float32) + b_router.astype(jnp.float32)
    top_val, top_idx = lax.top_k(logits, K)
    gates = jax.nn.softmax(top_val, axis=-1)
    e_flat = top_idx.reshape(-1)
    tok_flat = jnp.arange(T * K, dtype=jnp.int32) // K
    g_flat = gates.reshape(-1)
    order = jnp.argsort(e_flat, stable=True)
    e_sorted = e_flat[order]
    tok_sorted = tok_flat[order]
    g_sorted = g_flat[order]
    counts = jnp.zeros((E,), jnp.int32).at[e_flat].add(1)
    padded = (counts + MB - 1) // MB * MB
    start_unpadded = jnp.cumsum(counts) - counts
    pad_end = jnp.cumsum(padded)
    pad_start = pad_end - padded
    rank = jnp.arange(T * K, dtype=jnp.int32) - start_unpadded[e_sorted]
    dest = pad_start[e_sorted] + rank
    P = T * K + E * MB
    n_blk = P // MB
    x_buf = jnp.zeros((P, D), x.dtype).at[dest].set(xf[tok_sorted])
    blk_expert = jnp.minimum(jnp.searchsorted(pad_end, jnp.arange(n_blk, dtype=jnp.int32) * MB, side='right'), E - 1)

    def expert_block(args):
        xb, e = args
        h = xb @ w1[e] + b1[e]
        glu, lin = jnp.split(h, 2, axis=-1)
        glu = jnp.minimum(glu, SWIGLU_LIMIT)
        lin = jnp.clip(lin, -SWIGLU_LIMIT, SWIGLU_LIMIT)
        act = glu * jax.nn.sigmoid(SWIGLU_ALPHA * glu) * (lin + 1.0)
        return act @ w2[e] + b2[e]

    y_buf = lax.map(expert_block, (x_buf.reshape(n_blk, MB, D), blk_expert)).reshape(P, D)
    contrib = y_buf[dest] * g_sorted[:, None].astype(x.dtype)
    y = jnp.zeros((T, D), x.dtype).at[tok_sorted].add(contrib)
    return y.reshape(B, S, D)


def setup_inputs(seed: int = 0) -> dict:
    key = jax.random.key(seed)
    ks = jax.random.split(key, 24)
    D, E, F = D_MODEL, N_EXPERTS, D_FF
    std = D ** -0.5
    beta = DEEPNORM_BETA
    x = jax.random.normal(ks[0], (BATCH, SEQ, D), jnp.float32)
    positions = jnp.broadcast_to(jnp.arange(SEQ, dtype=jnp.int32), (BATCH, SEQ))
    v_col_scale = jnp.concatenate([jnp.ones((2 * D,), jnp.float32), jnp.full((D,), beta, jnp.float32)])
    da_w_in = jax.random.normal(ks[1], (N_DA_LAYERS, D, 3 * D), jnp.float32) * std * v_col_scale
    da_lambda_q1 = jax.random.normal(ks[2], (N_DA_LAYERS, DA_HEAD_DIM), jnp.float32) * 0.1
    da_lambda_k1 = jax.random.normal(ks[3], (N_DA_LAYERS, DA_HEAD_DIM), jnp.float32) * 0.1
    da_lambda_q2 = jax.random.normal(ks[4], (N_DA_LAYERS, DA_HEAD_DIM), jnp.float32) * 0.1
    da_lambda_k2 = jax.random.normal(ks[5], (N_DA_LAYERS, DA_HEAD_DIM), jnp.float32) * 0.1
    da_subln_g = 1.0 + 0.02 * jax.random.normal(ks[6], (N_DA_LAYERS, DA_V_DIM), jnp.float32)
    da_w_out = jax.random.normal(ks[7], (N_DA_LAYERS, D, D), jnp.float32) * std * beta
    sb_w_in = jax.random.normal(ks[8], (N_SB_LAYERS, D, 3 * D), jnp.float32) * std * v_col_scale
    sb_w_out = jax.random.normal(ks[9], (N_SB_LAYERS, D, D), jnp.float32) * std * beta
    ln1_g = 1.0 + 0.02 * jax.random.normal(ks[10], (DEPTH, D), jnp.float32)
    ln1_b = 0.02 * jax.random.normal(ks[11], (DEPTH, D), jnp.float32)
    ln2_g = 1.0 + 0.02 * jax.random.normal(ks[12], (DEPTH, D), jnp.float32)
    ln2_b = 0.02 * jax.random.normal(ks[13], (DEPTH, D), jnp.float32)
    router_w = jax.random.normal(ks[14], (DEPTH, D, E), jnp.float32) * std
    router_b = ROUTER_BIAS_SCALE * jax.random.normal(ks[15], (DEPTH, E), jnp.float32)
    expert_w1 = jax.random.normal(ks[16], (DEPTH, E, D, 2 * F), jnp.float32) * (std * beta)
    expert_b1 = 0.02 * jax.random.normal(ks[17], (DEPTH, E, 2 * F), jnp.float32)
    expert_w2 = jax.random.normal(ks[18], (DEPTH, E, F, D), jnp.float32) * (F ** -0.5 * beta)
    expert_b2 = 0.02 * jax.random.normal(ks[19], (DEPTH, E, D), jnp.float32)
    return {"x": x, "positions": positions,
            "da_w_in": da_w_in, "da_lambda_q1": da_lambda_q1, "da_lambda_k1": da_lambda_k1,
            "da_lambda_q2": da_lambda_q2, "da_lambda_k2": da_lambda_k2, "da_subln_g": da_subln_g,
            "da_w_out": da_w_out, "sb_w_in": sb_w_in, "sb_w_out": sb_w_out,
            "ln1_g": ln1_g, "ln1_b": ln1_b, "ln2_g": ln2_g, "ln2_b": ln2_b,
            "router_w": router_w, "router_b": router_b,
            "expert_w1": expert_w1, "expert_b1": expert_b1, "expert_w2": expert_w2, "expert_b2": expert_b2}


def reference(x, positions, da_w_in, da_lambda_q1, da_lambda_k1, da_lambda_q2, da_lambda_k2, da_subln_g,
              da_w_out, sb_w_in, sb_w_out, ln1_g, ln1_b, ln2_g, ln2_b, router_w, router_b,
              expert_w1, expert_b1, expert_w2, expert_b2):
    cos, sin = rope_cos_sin(positions, DA_HEAD_DIM)
    h = x
    for i in range(DEPTH):
        j = i // N_MIXERS
        if i % N_MIXERS == 0:
            lambda_init = 0.8 - 0.6 * math.exp(-0.3 * i)
            mix = diff_attention(h, cos, sin, da_w_in[j], da_lambda_q1[j], da_lambda_k1[j], da_lambda_q2[j],
                                 da_lambda_k2[j], da_subln_g[j], da_w_out[j], lambda_init)
        else:
            mix = stick_breaking_attention(h, sb_w_in[j], sb_w_out[j])
        h = layer_norm(DEEPNORM_ALPHA * h + mix, ln1_g[i], ln1_b[i])
        ffn = moe_ffn(h, router_w[i], router_b[i], expert_w1[i], expert_b1[i], expert_w2[i], expert_b2[i])
        h = layer_norm(DEEPNORM_ALPHA * h + ffn, ln2_g[i], ln2_b[i])
    return h
```

```python
import functools
import math

import jax
import jax.numpy as jnp
from jax import lax
from jax.experimental import pallas as pl
from jax.experimental.pallas import tpu as pltpu

F32 = jnp.float32
BF16 = jnp.bfloat16

HEAD_DIM = 128
DA_V_DIM = 2 * HEAD_DIM
TOP_K = 4
SWIGLU_ALPHA = 1.702
SWIGLU_LIMIT = 7.0
ROPE_THETA = 10000.0
LN_EPS = 1e-5

LANES = 128
VMEM_LIMIT_BYTES = 56 * 1024 * 1024

ROW_BLOCK = 256
CHUNK_SUBS = 6


def _cparams(sem):
    return pltpu.CompilerParams(dimension_semantics=sem, vmem_limit_bytes=VMEM_LIMIT_BYTES)


def _layer_norm_rows(z, g, b):
    mu = jnp.mean(z, axis=-1, keepdims=True)
    zc = z - mu
    var = jnp.mean(zc * zc, axis=-1, keepdims=True)
    return zc * lax.rsqrt(var + LN_EPS) * g + b


def _inproj_kernel(x_ref, w_ref, tab_ref, o_ref, xb_ref, *, rope, qk_tiles, q_tiles, q_scale):
    j = pl.program_id(1)

    @pl.when(j == 0)
    def _():
        xb_ref[...] = x_ref[...].astype(BF16)

    acc = jnp.dot(xb_ref[...], w_ref[...], preferred_element_type=F32)
    tn = acc.shape[1]

    if rope:
        @pl.when(j < qk_tiles)
        def _():
            cos = tab_ref[0]
            sin = tab_ref[1]
            for c in range(tn // LANES):
                xc = acc[:, c * LANES:(c + 1) * LANES]
                rot = pltpu.roll(xc, LANES // 2, 1)
                o_ref[:, c * LANES:(c + 1) * LANES] = (xc * cos + rot * sin).astype(o_ref.dtype)

        @pl.when(j >= qk_tiles)
        def _():
            o_ref[...] = acc.astype(o_ref.dtype)
    else:
        @pl.when(j < q_tiles)
        def _():
            o_ref[...] = (acc * q_scale).astype(o_ref.dtype)

        @pl.when(j >= q_tiles)
        def _():
            o_ref[...] = acc.astype(o_ref.dtype)


def _in_projection(x2d, w_bf16, tab, *, rope, d_model, q_scale, tm, tn):
    m, k = x2d.shape
    n = w_bf16.shape[1]
    q_tiles = d_model // tn
    qk_tiles = 2 * q_tiles
    kern = functools.partial(_inproj_kernel, rope=rope, qk_tiles=qk_tiles, q_tiles=q_tiles, q_scale=q_scale)
    return pl.pallas_call(
        kern,
        grid=(m // tm, n // tn),
        in_specs=[
            pl.BlockSpec((tm, k), lambda i, j: (i, 0)),
            pl.BlockSpec((k, tn), lambda i, j: (0, j)),
            pl.BlockSpec((None, 2, tm, LANES), lambda i, j: (jnp.minimum(j // q_tiles, 1), 0, i, 0)),
        ],
        out_specs=pl.BlockSpec((tm, tn), lambda i, j: (i, j)),
        out_shape=jax.ShapeDtypeStruct((m, n), BF16),
        scratch_shapes=[pltpu.VMEM((tm, k), BF16)],
        compiler_params=_cparams(("parallel", "arbitrary")),
        name="in_projection",
    )(x2d, w_bf16, tab)


def _da_kernel(q_ref, k_ref, v_ref, lam_ref, g_ref, o_ref, acc_ref, m_ref, l_ref, *, tq, lambda_init):
    i = pl.program_id(2)
    m_ref[...] = jnp.full(m_ref.shape, -jnp.inf, F32)
    l_ref[...] = jnp.zeros(l_ref.shape, F32)
    acc_ref[...] = jnp.zeros(acc_ref.shape, F32)
    q = q_ref[...]

    def step(j, masked):
        off = pl.multiple_of(j * tq, tq)
        ks = k_ref[pl.ds(off, tq), :]
        vs = v_ref[pl.ds(off, tq), :]
        for c in range(2):
            s = lax.dot_general(q[:, c * HEAD_DIM:(c + 1) * HEAD_DIM], ks[:, c * HEAD_DIM:(c + 1) * HEAD_DIM],
                                (((1,), (1,)), ((), ())), preferred_element_type=F32)
            if masked:
                row = lax.broadcasted_iota(jnp.int32, s.shape, 0)
                col = lax.broadcasted_iota(jnp.int32, s.shape, 1)
                s = jnp.where(col <= row, s, -jnp.inf)
            m_old = m_ref[c]
            m_new = jnp.maximum(m_old, jnp.max(s, axis=-1, keepdims=True))
            alpha = jnp.exp(m_old - m_new)
            p = jnp.exp(s - m_new)
            l_ref[c] = alpha * l_ref[c] + jnp.sum(p, axis=-1, keepdims=True)
            acc_ref[c] = alpha * acc_ref[c] + jnp.dot(p.astype(BF16), vs, preferred_element_type=F32)
            m_ref[c] = m_new

    def body(j, carry):
        step(j, False)
        return carry

    lax.fori_loop(0, i, body, 0)
    step(i, True)

    lam_rows = lam_ref[...]
    lam = (jnp.exp(jnp.sum(lam_rows[0:1] * lam_rows[1:2], axis=-1, keepdims=True))
           - jnp.exp(jnp.sum(lam_rows[2:3] * lam_rows[3:4], axis=-1, keepdims=True)) + lambda_init)
    o = acc_ref[0] / l_ref[0] - lam * (acc_ref[1] / l_ref[1])
    o = o * lax.rsqrt(jnp.mean(o * o, axis=-1, keepdims=True) + LN_EPS) * g_ref[...]
    o_ref[...] = (o * (1.0 - lambda_init)).astype(o_ref.dtype)


def _diff_attention(proj, lam_rows, subln_g, *, lambda_init, tq):
    b, s, n3 = proj.shape
    d_model = n3 // 3
    heads = d_model // DA_V_DIM
    kern = functools.partial(_da_kernel, tq=tq, lambda_init=lambda_init)
    return pl.pallas_call(
        kern,
        grid=(b, heads, s // tq),
        in_specs=[
            pl.BlockSpec((None, tq, DA_V_DIM), lambda bi, h, i: (bi, i, h)),
            pl.BlockSpec((None, s, DA_V_DIM), lambda bi, h, i: (bi, 0, heads + h)),
            pl.BlockSpec((None, s, DA_V_DIM), lambda bi, h, i: (bi, 0, 2 * heads + h)),
            pl.BlockSpec((4, HEAD_DIM), lambda bi, h, i: (0, 0)),
            pl.BlockSpec((1, DA_V_DIM), lambda bi, h, i: (0, 0)),
        ],
        out_specs=pl.BlockSpec((None, tq, DA_V_DIM), lambda bi, h, i: (bi, i, h)),
        out_shape=jax.ShapeDtypeStruct((b, s, d_model), BF16),
        scratch_shapes=[pltpu.VMEM((2, tq, DA_V_DIM), F32), pltpu.VMEM((2, tq, 1), F32),
                        pltpu.VMEM((2, tq, 1), F32)],
        compiler_params=_cparams(("parallel", "parallel", "arbitrary")),
        name="diff_attention",
    )(proj, proj, proj, lam_rows, subln_g)


def _sb_kernel(q_ref, k_ref, v_ref, o_ref, acc_ref, run_ref, *, tq):
    i = pl.program_id(2)
    acc_ref[...] = jnp.zeros(acc_ref.shape, F32)
    run_ref[...] = jnp.zeros(run_ref.shape, F32)
    q = q_ref[...]
    r_i = lax.broadcasted_iota(jnp.int32, (tq, tq), 0)
    c_i = lax.broadcasted_iota(jnp.int32, (tq, tq), 1)
    upper = (r_i > c_i).astype(BF16)

    def step(j, masked):
        off = pl.multiple_of(j * tq, tq)
        ks = k_ref[pl.ds(off, tq), :]
        vs = v_ref[pl.ds(off, tq), :]
        z = lax.dot_general(q, ks, (((1,), (1,)), ((), ())), preferred_element_type=F32)
        log_beta = jnp.minimum(z, 0.0) - jnp.log(1.0 + jnp.exp(-jnp.abs(z)))
        log_1m = log_beta - z
        if masked:
            strict = c_i < r_i
            log_1m = jnp.where(strict, log_1m, 0.0)
        hi = log_1m.astype(BF16)
        lo = (log_1m - hi.astype(F32)).astype(BF16)
        after = (jnp.dot(hi, upper, preferred_element_type=F32)
                 + jnp.dot(lo, upper, preferred_element_type=F32))
        run = run_ref[...]
        a = jnp.exp(log_beta + after + run)
        if masked:
            a = jnp.where(strict, a, 0.0)
        acc_ref[...] += jnp.dot(a.astype(BF16), vs, preferred_element_type=F32)
        run_ref[...] = run + jnp.sum(log_1m, axis=-1, keepdims=True)

    step(i, True)

    def body(jj, carry):
        step(i - 1 - jj, False)
        return carry

    lax.fori_loop(0, i, body, 0)
    o_ref[...] = acc_ref[...].astype(o_ref.dtype)


def _stick_breaking_attention(proj, *, tq):
    b, s, n3 = proj.shape
    d_model = n3 // 3
    heads = d_model // HEAD_DIM
    kern = functools.partial(_sb_kernel, tq=tq)
    return pl.pallas_call(
        kern,
        grid=(b, heads, s // tq),
        in_specs=[
            pl.BlockSpec((None, tq, HEAD_DIM), lambda bi, h, i: (bi, i, h)),
            pl.BlockSpec((None, s, HEAD_DIM), lambda bi, h, i: (bi, 0, heads + h)),
            pl.BlockSpec((None, s, HEAD_DIM), lambda bi, h, i: (bi, 0, 2 * heads + h)),
        ],
        out_specs=pl.BlockSpec((None, tq, HEAD_DIM), lambda bi, h, i: (bi, i, h)),
        out_shape=jax.ShapeDtypeStruct((b, s, d_model), BF16),
        scratch_shapes=[pltpu.VMEM((tq, HEAD_DIM), F32), pltpu.VMEM((tq, 1), F32)],
        compiler_params=_cparams(("parallel", "parallel", "arbitrary")),
        name="stick_breaking_attention",
    )(proj, proj, proj)


def _outproj_ln_kernel(o_ref, w_ref, h_ref, g_ref, b_ref, out_ref, *, alpha):
    mix = jnp.dot(o_ref[...], w_ref[...], preferred_element_type=F32)
    out_ref[...] = _layer_norm_rows(alpha * h_ref[...] + mix, g_ref[...], b_ref[...])


def _out_projection_ln(o2d, w_bf16, h2d, g, b, *, alpha, tm):
    m, d = h2d.shape
    kern = functools.partial(_outproj_ln_kernel, alpha=alpha)
    return pl.pallas_call(
        kern,
        grid=(m // tm,),
        in_specs=[
            pl.BlockSpec((tm, d), lambda i: (i, 0)),
            pl.BlockSpec((d, d), lambda i: (0, 0)),
            pl.BlockSpec((tm, d), lambda i: (i, 0)),
            pl.BlockSpec((1, d), lambda i: (0, 0)),
            pl.BlockSpec((1, d), lambda i: (0, 0)),
        ],
        out_specs=pl.BlockSpec((tm, d), lambda i: (i, 0)),
        out_shape=jax.ShapeDtypeStruct((m, d), F32),
        compiler_params=_cparams(("parallel",)),
        name="out_projection_ln",
    )(o2d, w_bf16, h2d, g, b)


def _router_kernel(h_ref, w_ref, b_ref, idx_ref, gate_ref, rank_ref, cnt_ref, carry_ref, *, n_experts):
    step = pl.program_id(0)

    @pl.when(step == 0)
    def _():
        carry_ref[...] = jnp.zeros(carry_ref.shape, F32)

    tm = h_ref.shape[0]
    logits = jnp.dot(h_ref[...], w_ref[...], preferred_element_type=F32,
                     precision=lax.Precision.HIGHEST) + b_ref[...]
    lane = lax.broadcasted_iota(jnp.int32, (tm, LANES), 1)
    work = jnp.where(lane < n_experts, logits, -jnp.inf)

    vals, idxs = [], []
    onehot = jnp.zeros((tm, LANES), F32)
    for _ in range(TOP_K):
        mx = jnp.max(work, axis=-1, keepdims=True)
        ix = jnp.min(jnp.where(work == mx, lane, LANES), axis=-1, keepdims=True)
        sel = lane == ix
        vals.append(mx)
        idxs.append(ix)
        onehot = jnp.where(sel, 1.0, onehot)
        work = jnp.where(sel, -jnp.inf, work)

    exps = [jnp.exp(v - vals[0]) for v in vals]
    denom = exps[0]
    for e in exps[1:]:
        denom = denom + e

    r_i = lax.broadcasted_iota(jnp.int32, (tm, tm), 0)
    c_i = lax.broadcasted_iota(jnp.int32, (tm, tm), 1)
    lower = (c_i < r_i).astype(BF16)
    earlier = jnp.dot(lower, onehot.astype(BF16), preferred_element_type=F32) + carry_ref[...]

    idx_out = jnp.zeros((tm, LANES), jnp.int32)
    gate_out = jnp.zeros((tm, LANES), F32)
    rank_out = jnp.zeros((tm, LANES), jnp.int32)
    for k in range(TOP_K):
        rk = jnp.sum(jnp.where(lane == idxs[k], earlier, 0.0), axis=-1, keepdims=True)
        idx_out = jnp.where(lane == k, idxs[k], idx_out)
        gate_out = jnp.where(lane == k, exps[k] / denom, gate_out)
        rank_out = jnp.where(lane == k, rk.astype(jnp.int32), rank_out)
    idx_ref[...] = idx_out
    gate_ref[...] = gate_out
    rank_ref[...] = rank_out

    total = carry_ref[...] + jnp.sum(onehot, axis=0, keepdims=True)
    carry_ref[...] = total
    cnt_ref[...] = total.astype(jnp.int32)


def _router(h2d, w_pad, b_pad, *, n_experts, tm):
    t, d = h2d.shape
    kern = functools.partial(_router_kernel, n_experts=n_experts)
    tok_spec = pl.BlockSpec((tm, LANES), lambda i: (i, 0))
    return pl.pallas_call(
        kern,
        grid=(t // tm,),
        in_specs=[
            pl.BlockSpec((tm, d), lambda i: (i, 0)),
            pl.BlockSpec((d, LANES), lambda i: (0, 0)),
            pl.BlockSpec((1, LANES), lambda i: (0, 0)),
        ],
        out_specs=[tok_spec, tok_spec, tok_spec, pl.BlockSpec((1, LANES), lambda i: (0, 0))],
        out_shape=[jax.ShapeDtypeStruct((t, LANES), jnp.int32), jax.ShapeDtypeStruct((t, LANES), F32),
                   jax.ShapeDtypeStruct((t, LANES), jnp.int32), jax.ShapeDtypeStruct((1, LANES), jnp.int32)],
        scratch_shapes=[pltpu.VMEM((1, LANES), F32)],
        compiler_params=_cparams(("arbitrary",)),
        name="router",
    )(h2d, w_pad, b_pad)


def _dispatch_kernel(dest_ref, h_hbm, xin_hbm, x_hbm, sem, *, tm):
    del xin_hbm
    base = pl.program_id(0) * tm

    def row_copy(t, k):
        r = dest_ref[(base + t) * TOP_K + k]
        return pltpu.make_async_copy(h_hbm.at[pl.ds(base + t, 1), :], x_hbm.at[pl.ds(r, 1), :], sem)

    def issue(t, carry):
        for k in range(TOP_K):
            row_copy(t, k).start()
        return carry

    def drain(t, carry):
        for k in range(TOP_K):
            row_copy(t, k).wait()
        return carry

    lax.fori_loop(0, tm, issue, 0)
    lax.fori_loop(0, tm, drain, 0)


def _dispatch(dest_flat, h2d, x_zero, *, tm):
    t, d = h2d.shape
    kern = functools.partial(_dispatch_kernel, tm=tm)
    return pl.pallas_call(
        kern,
        grid_spec=pltpu.PrefetchScalarGridSpec(
            num_scalar_prefetch=1,
            grid=(t // tm,),
            in_specs=[pl.BlockSpec(memory_space=pl.ANY), pl.BlockSpec(memory_space=pl.ANY)],
            out_specs=pl.BlockSpec(memory_space=pl.ANY),
            scratch_shapes=[pltpu.SemaphoreType.DMA(())],
        ),
        out_shape=jax.ShapeDtypeStruct(x_zero.shape, x_zero.dtype),
        input_output_aliases={2: 0},
        compiler_params=_cparams(("arbitrary",)),
        name="dispatch",
    )(dest_flat, h2d, x_zero)


def _expert_kernel(ce_ref, cr_ref, cn_ref, x_hbm, w1g_ref, w1l_ref, b1g_ref, b1l_ref, w2_ref, b2_ref, y_hbm,
                   xb_ref, y_ref, stage_ref, w1g_b, w1l_b, w2_b, sem, *, nf):
    del ce_ref
    c = pl.program_id(0)
    f = pl.program_id(1)
    ns = cn_ref[c]
    row0 = cr_ref[c]

    def rows(s):
        return pl.ds(pl.multiple_of(row0 + s * ROW_BLOCK, ROW_BLOCK), ROW_BLOCK)

    def x_copy(s):
        return pltpu.make_async_copy(x_hbm.at[rows(s), :], stage_ref, sem.at[0])

    def y_copy(s):
        return pltpu.make_async_copy(y_ref.at[s], y_hbm.at[rows(s), :], sem.at[1])

    @pl.when(f == 0)
    def _():
        def load(s, carry):
            cp = x_copy(s)
            cp.start()
            cp.wait()
            xb_ref[s] = stage_ref[...].astype(BF16)
            return carry
        lax.fori_loop(0, ns, load, 0)

    @pl.when(ns > 0)
    def _():
        w1g_b[...] = w1g_ref[...].astype(BF16)
        w1l_b[...] = w1l_ref[...].astype(BF16)
        w2_b[...] = w2_ref[...].astype(BF16)

    def sub(s, carry):
        xs = xb_ref[s]
        hg = jnp.dot(xs, w1g_b[...], preferred_element_type=F32) + b1g_ref[...]
        hl = jnp.dot(xs, w1l_b[...], preferred_element_type=F32) + b1l_ref[...]
        glu = jnp.minimum(hg, SWIGLU_LIMIT)
        lin = jnp.clip(hl, -SWIGLU_LIMIT, SWIGLU_LIMIT)
        act = glu * jax.nn.sigmoid(SWIGLU_ALPHA * glu) * (lin + 1.0)
        contrib = jnp.dot(act.astype(BF16), w2_b[...], preferred_element_type=F32)

        @pl.when(f == 0)
        def _():
            y_ref[s] = contrib + b2_ref[...]

        @pl.when(f > 0)
        def _():
            y_ref[s] += contrib
        return carry

    lax.fori_loop(0, ns, sub, 0)

    @pl.when(f == nf - 1)
    def _():
        def store(s, carry):
            cp = y_copy(s)
            cp.start()
            cp.wait()
            return carry
        lax.fori_loop(0, ns, store, 0)


def _expert_ffn(chunk_e, chunk_row, chunk_ns, x_buf, w1, b1, w2, b2, *, layer, tf):
    p, d = x_buf.shape
    n_experts, _, f2 = w1.shape[1:]
    d_ff = f2 // 2
    nf = d_ff // tf
    nc = chunk_e.shape[0]
    b1r = b1.reshape(b1.shape[0], n_experts, 1, f2)
    b2r = b2.reshape(b2.shape[0], n_experts, 1, d)

    def f_eff(c, f, cn):
        return jnp.where(cn[c] > 0, f, nf - 1)

    kern = functools.partial(_expert_kernel, nf=nf)
    return pl.pallas_call(
        kern,
        grid_spec=pltpu.PrefetchScalarGridSpec(
            num_scalar_prefetch=3,
            grid=(nc, nf),
            in_specs=[
                pl.BlockSpec(memory_space=pl.ANY),
                pl.BlockSpec((None, None, d, tf), lambda c, f, ce, cr, cn: (layer, ce[c], 0, f_eff(c, f, cn))),
                pl.BlockSpec((None, None, d, tf), lambda c, f, ce, cr, cn: (layer, ce[c], 0, nf + f_eff(c, f, cn))),
                pl.BlockSpec((None, None, 1, tf), lambda c, f, ce, cr, cn: (layer, ce[c], 0, f_eff(c, f, cn))),
                pl.BlockSpec((None, None, 1, tf), lambda c, f, ce, cr, cn: (layer, ce[c], 0, nf + f_eff(c, f, cn))),
                pl.BlockSpec((None, None, tf, d), lambda c, f, ce, cr, cn: (layer, ce[c], f_eff(c, f, cn), 0)),
                pl.BlockSpec((None, None, 1, d), lambda c, f, ce, cr, cn: (layer, ce[c], 0, 0)),
            ],
            out_specs=pl.BlockSpec(memory_space=pl.ANY),
            scratch_shapes=[
                pltpu.VMEM((CHUNK_SUBS, ROW_BLOCK, d), BF16),
                pltpu.VMEM((CHUNK_SUBS, ROW_BLOCK, d), F32),
                pltpu.VMEM((ROW_BLOCK, d), F32),
                pltpu.VMEM((d, tf), BF16),
                pltpu.VMEM((d, tf), BF16),
                pltpu.VMEM((tf, d), BF16),
                pltpu.SemaphoreType.DMA((2,)),
            ],
        ),
        out_shape=jax.ShapeDtypeStruct((p, d), F32),
        input_output_aliases={3: 0},
        compiler_params=_cparams(("arbitrary", "arbitrary")),
        name="expert_ffn",
    )(chunk_e, chunk_row, chunk_ns, x_buf, w1, w1, b1r, b1r, w2, b2r)


def _combine_kernel(dest_ref, y_hbm, gate_ref, h_ref, g_ref, b_ref, out_ref, buf_ref, sem, *, tm, alpha):
    base = pl.program_id(0) * tm

    def row_copy(t, k):
        r = dest_ref[(base + t) * TOP_K + k]
        return pltpu.make_async_copy(y_hbm.at[pl.ds(r, 1), :], buf_ref.at[k, pl.ds(t, 1), :], sem)

    def issue(t, carry):
        for k in range(TOP_K):
            row_copy(t, k).start()
        return carry

    def drain(t, carry):
        for k in range(TOP_K):
            row_copy(t, k).wait()
        return carry

    lax.fori_loop(0, tm, issue, 0)
    lax.fori_loop(0, tm, drain, 0)

    gates = gate_ref[...]
    y = gates[:, 0:1] * buf_ref[0]
    for k in range(1, TOP_K):
        y = y + gates[:, k:k + 1] * buf_ref[k]
    out_ref[...] = _layer_norm_rows(alpha * h_ref[...] + y, g_ref[...], b_ref[...])


def _combine_ln(dest_flat, y_buf, gates, h2d, g, b, *, alpha, tm):
    t, d = h2d.shape
    kern = functools.partial(_combine_kernel, tm=tm, alpha=alpha)
    return pl.pallas_call(
        kern,
        grid_spec=pltpu.PrefetchScalarGridSpec(
            num_scalar_prefetch=1,
            grid=(t // tm,),
            in_specs=[
                pl.BlockSpec(memory_space=pl.ANY),
                pl.BlockSpec((tm, LANES), lambda i, dest: (i, 0)),
                pl.BlockSpec((tm, d), lambda i, dest: (i, 0)),
                pl.BlockSpec((1, d), lambda i, dest: (0, 0)),
                pl.BlockSpec((1, d), lambda i, dest: (0, 0)),
            ],
            out_specs=pl.BlockSpec((tm, d), lambda i, dest: (i, 0)),
            scratch_shapes=[pltpu.VMEM((TOP_K, tm, d), F32), pltpu.SemaphoreType.DMA(())],
        ),
        out_shape=jax.ShapeDtypeStruct((t, d), F32),
        compiler_params=_cparams(("arbitrary",)),
        name="combine_ln",
    )(dest_flat, y_buf, gates, h2d, g, b)


def _routing_tables(idx, rank, counts, *, n_chunks):
    n_experts = counts.shape[0]
    padded = (counts + ROW_BLOCK - 1) // ROW_BLOCK * ROW_BLOCK
    pad_end = jnp.cumsum(padded)
    pad_start = pad_end - padded
    dest = pad_start[idx] + rank
    nsub = padded // ROW_BLOCK
    nchunk = (nsub + CHUNK_SUBS - 1) // CHUNK_SUBS
    chunk_end = jnp.cumsum(nchunk)
    chunk_start = chunk_end - nchunk
    total = chunk_end[-1]
    cids = jnp.arange(n_chunks, dtype=jnp.int32)
    clamped = jnp.minimum(cids, total - 1)
    ce = jnp.minimum(jnp.searchsorted(chunk_end, clamped, side="right"), n_experts - 1).astype(jnp.int32)
    local = clamped - chunk_start[ce]
    crow = (pad_start[ce] + local * (CHUNK_SUBS * ROW_BLOCK)).astype(jnp.int32)
    cns = jnp.clip(nsub[ce] - local * CHUNK_SUBS, 0, CHUNK_SUBS)
    cns = jnp.where(cids < total, cns, 0).astype(jnp.int32)
    return dest.reshape(-1).astype(jnp.int32), ce, crow, cns


def _moe_ln(h2d, router_w, router_b, w1, b1, w2, b2, g, b, *, layer, alpha, tiles):
    t, d = h2d.shape
    n_experts = router_w.shape[1]
    w_pad = jnp.pad(router_w, ((0, 0), (0, LANES - n_experts)))
    b_pad = jnp.pad(router_b, (0, LANES - n_experts)).reshape(1, LANES)
    idx, gates, rank, counts = _router(h2d, w_pad, b_pad, n_experts=n_experts, tm=tiles["router_tm"])
    total_sub = t * TOP_K // ROW_BLOCK + n_experts
    n_chunks = total_sub // CHUNK_SUBS + n_experts
    dest_flat, ce, crow, cns = _routing_tables(idx[:, :TOP_K], rank[:, :TOP_K], counts[0, :n_experts],
                                               n_chunks=n_chunks)
    p_rows = total_sub * ROW_BLOCK
    x_buf = _dispatch(dest_flat, h2d, jnp.zeros((p_rows, d), F32), tm=tiles["dispatch_tm"])
    y_buf = _expert_ffn(ce, crow, cns, x_buf, w1, b1, w2, b2, layer=layer, tf=tiles["expert_tf"])
    return _combine_ln(dest_flat, y_buf, gates, h2d, g, b, alpha=alpha, tm=tiles["combine_tm"])


def _rope_tables(positions, q_scale):
    inv_freq = ROPE_THETA ** (-jnp.arange(0, HEAD_DIM, 2, dtype=F32) / HEAD_DIM)
    ang = positions.astype(F32).reshape(-1)[:, None] * inv_freq
    cos, sin = jnp.cos(ang), jnp.sin(ang)
    cos2 = jnp.concatenate([cos, cos], axis=-1)
    sin2 = jnp.concatenate([-sin, sin], axis=-1)
    k_tab = jnp.stack([cos2, sin2])
    return jnp.stack([k_tab * q_scale, k_tab])


def _tiles(t, s, d, d_ff):
    return {
        "inproj_tm": min(1024, t), "inproj_tn": min(1024, d),
        "attn_tq": min(256, s),
        "outproj_tm": min(256, t),
        "router_tm": min(256, t),
        "dispatch_tm": min(256, t),
        "combine_tm": min(128, t),
        "expert_tf": min(256, d_ff),
    }


def kernel(x, positions, da_w_in, da_lambda_q1, da_lambda_k1, da_lambda_q2, da_lambda_k2, da_subln_g, da_w_out,
           sb_w_in, sb_w_out, ln1_g, ln1_b, ln2_g, ln2_b, router_w, router_b, expert_w1, expert_b1, expert_w2,
           expert_b2):
    bsz, seq, d = x.shape
    t = bsz * seq
    depth = ln1_g.shape[0]
    d_ff = expert_w2.shape[2]
    alpha = (2 * depth) ** 0.25
    q_scale = HEAD_DIM ** -0.5
    tiles = _tiles(t, seq, d, d_ff)
    tab = _rope_tables(positions, q_scale)

    h = x.reshape(t, d)
    for i in range(depth):
        j = i // 2
        if i % 2 == 0:
            lambda_init = 0.8 - 0.6 * math.exp(-0.3 * i)
            proj = _in_projection(h, da_w_in[j].astype(BF16), tab, rope=True, d_model=d, q_scale=q_scale,
                                  tm=tiles["inproj_tm"], tn=tiles["inproj_tn"])
            lam_rows = jnp.stack([da_lambda_q1[j], da_lambda_k1[j], da_lambda_q2[j], da_lambda_k2[j]])
            o = _diff_attention(proj.reshape(bsz, seq, 3 * d), lam_rows, da_subln_g[j].reshape(1, DA_V_DIM),
                                lambda_init=lambda_init, tq=tiles["attn_tq"])
            w_out = da_w_out[j]
        else:
            proj = _in_projection(h, sb_w_in[j].astype(BF16), tab, rope=False, d_model=d, q_scale=q_scale,
                                  tm=tiles["inproj_tm"], tn=tiles["inproj_tn"])
            o = _stick_breaking_attention(proj.reshape(bsz, seq, 3 * d), tq=tiles["attn_tq"])
            w_out = sb_w_out[j]
        h = _out_projection_ln(o.reshape(t, d), w_out.astype(BF16), h, ln1_g[i].reshape(1, d),
                               ln1_b[i].reshape(1, d), alpha=alpha, tm=tiles["outproj_tm"])
        h = _moe_ln(h, router_w[i], router_b[i], expert_w1, expert_b1, expert_w2, expert_b2,
                    ln2_g[i].reshape(1, d), ln2_b[i].reshape(1, d), layer=i, alpha=alpha, tiles=tiles)
    return h.reshape(bsz, seq, d)
```

```python
import functools
import math

import jax
import jax.numpy as jnp
from jax import lax
from jax.experimental import pallas as pl
from jax.experimental.pallas import tpu as pltpu

F32 = jnp.float32
BF16 = jnp.bfloat16

HEAD_DIM = 128
DA_V_DIM = 2 * HEAD_DIM
TOP_K = 4
SWIGLU_ALPHA = 1.702
SWIGLU_LIMIT = 7.0
ROPE_THETA = 10000.0
LN_EPS = 1e-5

LANES = 128
SUBLANES = 8
VMEM_LIMIT_BYTES = 56 * 1024 * 1024

CHUNK_ROWS = 1152
SMALL_ROWS = 128


def _cparams(sem):
    return pltpu.CompilerParams(dimension_semantics=sem, vmem_limit_bytes=VMEM_LIMIT_BYTES)


def _layer_norm_rows(z, g, b):
    mu = jnp.mean(z, axis=-1, keepdims=True)
    zc = z - mu
    var = jnp.mean(zc * zc, axis=-1, keepdims=True)
    return zc * lax.rsqrt(var + LN_EPS) * g + b


def _inproj_kernel(x_ref, w_ref, tab_ref, o_ref, xb_ref, *, rope, qk_tiles, q_tiles, q_scale):
    j = pl.program_id(1)

    @pl.when(j == 0)
    def _():
        xb_ref[...] = x_ref[...].astype(BF16)

    acc = jnp.dot(xb_ref[...], w_ref[...], preferred_element_type=F32)
    tn = acc.shape[1]

    if rope:
        @pl.when(j < qk_tiles)
        def _():
            cos = tab_ref[0]
            sin = tab_ref[1]
            for c in range(tn // LANES):
                xc = acc[:, c * LANES:(c + 1) * LANES]
                rot = pltpu.roll(xc, LANES // 2, 1)
                o_ref[:, c * LANES:(c + 1) * LANES] = (xc * cos + rot * sin).astype(o_ref.dtype)

        @pl.when(j >= qk_tiles)
        def _():
            o_ref[...] = acc.astype(o_ref.dtype)
    else:
        @pl.when(j < q_tiles)
        def _():
            o_ref[...] = (acc * q_scale).astype(o_ref.dtype)

        @pl.when(j >= q_tiles)
        def _():
            o_ref[...] = acc.astype(o_ref.dtype)


def _in_projection(x2d, w_bf16, tab, *, rope, d_model, q_scale, tm, tn):
    m, k = x2d.shape
    n = w_bf16.shape[1]
    q_tiles = d_model // tn
    qk_tiles = 2 * q_tiles
    kern = functools.partial(_inproj_kernel, rope=rope, qk_tiles=qk_tiles, q_tiles=q_tiles, q_scale=q_scale)
    return pl.pallas_call(
        kern,
        grid=(m // tm, n // tn),
        in_specs=[
            pl.BlockSpec((tm, k), lambda i, j: (i, 0)),
            pl.BlockSpec((k, tn), lambda i, j: (0, j)),
            pl.BlockSpec((None, 2, tm, LANES), lambda i, j: (jnp.minimum(j // q_tiles, 1), 0, i, 0)),
        ],
        out_specs=pl.BlockSpec((tm, tn), lambda i, j: (i, j)),
        out_shape=jax.ShapeDtypeStruct((m, n), BF16),
        scratch_shapes=[pltpu.VMEM((tm, k), BF16)],
        compiler_params=_cparams(("parallel", "arbitrary")),
        name="in_projection",
    )(x2d, w_bf16, tab)


def _transpose_bf16(x):
    return x.astype(F32).T.astype(BF16)


def _store_v_transposed(v_ref, vt_ref, tk):
    for n in range(vt_ref.shape[0]):
        vt_ref[n] = _transpose_bf16(v_ref[n * tk:(n + 1) * tk, :])


def _da_kernel(q_ref, k_ref, v_ref, lam_ref, g_ref, o_ref, vt_ref, acc_ref, m_ref, l_ref, *, tq, lambda_init):
    i = pl.program_id(2)

    @pl.when(i == 0)
    def _():
        _store_v_transposed(v_ref, vt_ref, tq)

    m_ref[...] = jnp.full(m_ref.shape, -jnp.inf, F32)
    l_ref[...] = jnp.zeros(l_ref.shape, F32)
    acc_ref[...] = jnp.zeros(acc_ref.shape, F32)
    q_t = _transpose_bf16(q_ref[...])

    def step(j, masked):
        off = pl.multiple_of(j * tq, tq)
        ks = k_ref[pl.ds(off, tq), :]
        v_t = vt_ref[j]
        for c in range(2):
            s = jnp.dot(ks[:, c * HEAD_DIM:(c + 1) * HEAD_DIM], q_t[c * HEAD_DIM:(c + 1) * HEAD_DIM, :],
                        preferred_element_type=F32)
            if masked:
                key = lax.broadcasted_iota(jnp.int32, s.shape, 0)
                qry = lax.broadcasted_iota(jnp.int32, s.shape, 1)
                s = jnp.where(key <= qry, s, -jnp.inf)
            m_old = m_ref[c]
            m_new = jnp.maximum(m_old, jnp.max(s, axis=0, keepdims=True))
            alpha = jnp.exp(m_old - m_new)
            p = jnp.exp(s - m_new)
            l_ref[c] = alpha * l_ref[c] + jnp.sum(p, axis=0, keepdims=True)
            acc_ref[c] = alpha * acc_ref[c] + jnp.dot(v_t, p.astype(BF16), preferred_element_type=F32)
            m_ref[c] = m_new

    def body(j, carry):
        step(j, False)
        return carry

    lax.fori_loop(0, i, body, 0)
    step(i, True)

    lam_rows = lam_ref[...]
    lam = (jnp.exp(jnp.sum(lam_rows[0:1] * lam_rows[1:2], axis=-1, keepdims=True))
           - jnp.exp(jnp.sum(lam_rows[2:3] * lam_rows[3:4], axis=-1, keepdims=True)) + lambda_init)
    o_t = acc_ref[0] / l_ref[0] - lam * (acc_ref[1] / l_ref[1])
    o = o_t.T
    o = o * lax.rsqrt(jnp.mean(o * o, axis=-1, keepdims=True) + LN_EPS) * g_ref[...]
    o_ref[...] = (o * (1.0 - lambda_init)).astype(o_ref.dtype)


def _diff_attention(proj, lam_rows, subln_g, *, lambda_init, tq):
    b, s, n3 = proj.shape
    d_model = n3 // 3
    heads = d_model // DA_V_DIM
    kern = functools.partial(_da_kernel, tq=tq, lambda_init=lambda_init)
    return pl.pallas_call(
        kern,
        grid=(b, heads, s // tq),
        in_specs=[
            pl.BlockSpec((None, tq, DA_V_DIM), lambda bi, h, i: (bi, i, h)),
            pl.BlockSpec((None, s, DA_V_DIM), lambda bi, h, i: (bi, 0, heads + h)),
            pl.BlockSpec((None, s, DA_V_DIM), lambda bi, h, i: (bi, 0, 2 * heads + h)),
            pl.BlockSpec((4, HEAD_DIM), lambda bi, h, i: (0, 0)),
            pl.BlockSpec((1, DA_V_DIM), lambda bi, h, i: (0, 0)),
        ],
        out_specs=pl.BlockSpec((None, tq, DA_V_DIM), lambda bi, h, i: (bi, i, h)),
        out_shape=jax.ShapeDtypeStruct((b, s, d_model), BF16),
        scratch_shapes=[pltpu.VMEM((s // tq, DA_V_DIM, tq), BF16), pltpu.VMEM((2, DA_V_DIM, tq), F32),
                        pltpu.VMEM((2, 1, tq), F32), pltpu.VMEM((2, 1, tq), F32)],
        compiler_params=_cparams(("parallel", "parallel", "arbitrary")),
        name="diff_attention",
    )(proj, proj, proj, lam_rows, subln_g)


def _sb_kernel(q_ref, k_ref, v_ref, o_ref, vt_ref, acc_ref, run_ref, *, tq):
    i = pl.program_id(2)

    @pl.when(i == 0)
    def _():
        _store_v_transposed(v_ref, vt_ref, tq)

    acc_ref[...] = jnp.zeros(acc_ref.shape, F32)
    run_ref[...] = jnp.zeros(run_ref.shape, F32)
    q_t = _transpose_bf16(q_ref[...])
    key = lax.broadcasted_iota(jnp.int32, (tq, tq), 0)
    qry = lax.broadcasted_iota(jnp.int32, (tq, tq), 1)
    later = (qry > key).astype(BF16)

    def step(j, masked):
        off = pl.multiple_of(j * tq, tq)
        ks = k_ref[pl.ds(off, tq), :]
        v_t = vt_ref[j]
        z = jnp.dot(ks, q_t, preferred_element_type=F32)
        log_beta = jnp.minimum(z, 0.0) - jnp.log(1.0 + jnp.exp(-jnp.abs(z)))
        log_1m = log_beta - z
        if masked:
            strict = key < qry
            log_1m = jnp.where(strict, log_1m, 0.0)
        hi = log_1m.astype(BF16)
        lo = (log_1m - hi.astype(F32)).astype(BF16)
        after = (jnp.dot(later, hi, preferred_element_type=F32)
                 + jnp.dot(later, lo, preferred_element_type=F32))
        run = run_ref[...]
        a = jnp.exp(log_beta + after + run)
        if masked:
            a = jnp.where(strict, a, 0.0)
        acc_ref[...] += jnp.dot(v_t, a.astype(BF16), preferred_element_type=F32)
        run_ref[...] = run + jnp.sum(log_1m, axis=0, keepdims=True)

    step(i, True)

    def body(jj, carry):
        step(i - 1 - jj, False)
        return carry

    lax.fori_loop(0, i, body, 0)
    o_ref[...] = acc_ref[...].T.astype(o_ref.dtype)


def _stick_breaking_attention(proj, *, tq):
    b, s, n3 = proj.shape
    d_model = n3 // 3
    heads = d_model // HEAD_DIM
    kern = functools.partial(_sb_kernel, tq=tq)
    return pl.pallas_call(
        kern,
        grid=(b, heads, s // tq),
        in_specs=[
            pl.BlockSpec((None, tq, HEAD_DIM), lambda bi, h, i: (bi, i, h)),
            pl.BlockSpec((None, s, HEAD_DIM), lambda bi, h, i: (bi, 0, heads + h)),
            pl.BlockSpec((None, s, HEAD_DIM), lambda bi, h, i: (bi, 0, 2 * heads + h)),
        ],
        out_specs=pl.BlockSpec((None, tq, HEAD_DIM), lambda bi, h, i: (bi, i, h)),
        out_shape=jax.ShapeDtypeStruct((b, s, d_model), BF16),
        scratch_shapes=[pltpu.VMEM((s // tq, HEAD_DIM, tq), BF16), pltpu.VMEM((HEAD_DIM, tq), F32),
                        pltpu.VMEM((1, tq), F32)],
        compiler_params=_cparams(("parallel", "parallel", "arbitrary")),
        name="stick_breaking_attention",
    )(proj, proj, proj)


def _outproj_ln_kernel(o_ref, w_ref, h_ref, g_ref, b_ref, out_ref, *, alpha):
    mix = jnp.dot(o_ref[...], w_ref[...], preferred_element_type=F32)
    out_ref[...] = _layer_norm_rows(alpha * h_ref[...] + mix, g_ref[...], b_ref[...])


def _out_projection_ln(o2d, w_bf16, h2d, g, b, *, alpha, tm):
    m, d = h2d.shape
    kern = functools.partial(_outproj_ln_kernel, alpha=alpha)
    return pl.pallas_call(
        kern,
        grid=(m // tm,),
        in_specs=[
            pl.BlockSpec((tm, d), lambda i: (i, 0)),
            pl.BlockSpec((d, d), lambda i: (0, 0)),
            pl.BlockSpec((tm, d), lambda i: (i, 0)),
            pl.BlockSpec((1, d), lambda i: (0, 0)),
            pl.BlockSpec((1, d), lambda i: (0, 0)),
        ],
        out_specs=pl.BlockSpec((tm, d), lambda i: (i, 0)),
        out_shape=jax.ShapeDtypeStruct((m, d), F32),
        compiler_params=_cparams(("parallel",)),
        name="out_projection_ln",
    )(o2d, w_bf16, h2d, g, b)


def _router_kernel(h_ref, w_ref, b_ref, idx_ref, gate_ref, rank_ref, cnt_ref, carry_ref, *, n_experts):
    step = pl.program_id(0)

    @pl.when(step == 0)
    def _():
        carry_ref[...] = jnp.zeros(carry_ref.shape, F32)

    tm = h_ref.shape[0]
    logits = jnp.dot(h_ref[...], w_ref[...], preferred_element_type=F32,
                     precision=lax.Precision.HIGHEST) + b_ref[...]
    lane = lax.broadcasted_iota(jnp.int32, (tm, LANES), 1)
    work = jnp.where(lane < n_experts, logits, -jnp.inf)

    vals, idxs = [], []
    onehot = jnp.zeros((tm, LANES), F32)
    for _ in range(TOP_K):
        mx = jnp.max(work, axis=-1, keepdims=True)
        ix = jnp.min(jnp.where(work == mx, lane, LANES), axis=-1, keepdims=True)
        sel = lane == ix
        vals.append(mx)
        idxs.append(ix)
        onehot = jnp.where(sel, 1.0, onehot)
        work = jnp.where(sel, -jnp.inf, work)

    exps = [jnp.exp(v - vals[0]) for v in vals]
    denom = exps[0]
    for e in exps[1:]:
        denom = denom + e

    r_i = lax.broadcasted_iota(jnp.int32, (tm, tm), 0)
    c_i = lax.broadcasted_iota(jnp.int32, (tm, tm), 1)
    lower = (c_i < r_i).astype(BF16)
    earlier = jnp.dot(lower, onehot.astype(BF16), preferred_element_type=F32) + carry_ref[...]

    idx_out = jnp.zeros((tm, LANES), jnp.int32)
    gate_out = jnp.zeros((tm, LANES), F32)
    rank_out = jnp.zeros((tm, LANES), jnp.int32)
    for k in range(TOP_K):
        rk = jnp.sum(jnp.where(lane == idxs[k], earlier, 0.0), axis=-1, keepdims=True)
        idx_out = jnp.where(lane == k, idxs[k], idx_out)
        gate_out = jnp.where(lane == k, exps[k] / denom, gate_out)
        rank_out = jnp.where(lane == k, rk.astype(jnp.int32), rank_out)
    idx_ref[...] = idx_out
    gate_ref[...] = gate_out
    rank_ref[...] = rank_out

    total = carry_ref[...] + jnp.sum(onehot, axis=0, keepdims=True)
    carry_ref[...] = total
    cnt_ref[...] = total.astype(jnp.int32)


def _router(h2d, w_pad, b_pad, *, n_experts, tm):
    t, d = h2d.shape
    kern = functools.partial(_router_kernel, n_experts=n_experts)
    tok_spec = pl.BlockSpec((tm, LANES), lambda i: (i, 0))
    return pl.pallas_call(
        kern,
        grid=(t // tm,),
        in_specs=[
            pl.BlockSpec((tm, d), lambda i: (i, 0)),
            pl.BlockSpec((d, LANES), lambda i: (0, 0)),
            pl.BlockSpec((1, LANES), lambda i: (0, 0)),
        ],
        out_specs=[tok_spec, tok_spec, tok_spec, pl.BlockSpec((1, LANES), lambda i: (0, 0))],
        out_shape=[jax.ShapeDtypeStruct((t, LANES), jnp.int32), jax.ShapeDtypeStruct((t, LANES), F32),
                   jax.ShapeDtypeStruct((t, LANES), jnp.int32), jax.ShapeDtypeStruct((1, LANES), jnp.int32)],
        scratch_shapes=[pltpu.VMEM((1, LANES), F32)],
        compiler_params=_cparams(("arbitrary",)),
        name="router",
    )(h2d, w_pad, b_pad)


def _dispatch_kernel(dest_ref, h_ref, xin_hbm, x_hbm, sem, *, tm):
    del xin_hbm
    base = pl.program_id(0) * tm

    def row_copy(t, k):
        r = dest_ref[(base + t) * TOP_K + k]
        return pltpu.make_async_copy(h_ref.at[pl.ds(t, 1), :], x_hbm.at[pl.ds(r, 1), :], sem)

    def issue(t, carry):
        for k in range(TOP_K):
            row_copy(t, k).start()
        return carry

    def drain(t, carry):
        for k in range(TOP_K):
            row_copy(t, k).wait()
        return carry

    lax.fori_loop(0, tm, issue, 0)
    lax.fori_loop(0, tm, drain, 0)


def _dispatch(dest_flat, h2d, x_zero, *, tm):
    t, d = h2d.shape
    kern = functools.partial(_dispatch_kernel, tm=tm)
    return pl.pallas_call(
        kern,
        grid_spec=pltpu.PrefetchScalarGridSpec(
            num_scalar_prefetch=1,
            grid=(t // tm,),
            in_specs=[pl.BlockSpec((tm, d), lambda i, dest: (i, 0)), pl.BlockSpec(memory_space=pl.ANY)],
            out_specs=pl.BlockSpec(memory_space=pl.ANY),
            scratch_shapes=[pltpu.SemaphoreType.DMA(())],
        ),
        out_shape=jax.ShapeDtypeStruct(x_zero.shape, x_zero.dtype),
        input_output_aliases={2: 0},
        compiler_params=_cparams(("arbitrary",)),
        name="dispatch",
    )(dest_flat, h2d, x_zero)


def _expert_kernel(ce_ref, cr_ref, cn_ref, x_hbm, w1g_ref, w1l_ref, b1g_ref, b1l_ref, w2_ref, b2_ref, y_hbm,
                   xb_ref, y_ref, stage_ref, w1g_b, w1l_b, w2_b, sem, *, nf):
    del ce_ref
    c = pl.program_id(0)
    f = pl.program_id(1)
    nv = cn_ref[c]
    row0 = pl.multiple_of(cr_ref[c], SUBLANES)
    d = y_ref.shape[1]

    @pl.when((f == 0) & (nv > 0))
    def _():
        cp = pltpu.make_async_copy(x_hbm.at[pl.ds(row0, CHUNK_ROWS), :], stage_ref, sem.at[0])
        cp.start()
        cp.wait()
        xb_ref[...] = stage_ref[...].astype(BF16)

    @pl.when(nv > 0)
    def _():
        w1g_b[...] = w1g_ref[...].astype(BF16)
        w1l_b[...] = w1l_ref[...].astype(BF16)
        w2_b[...] = w2_ref[...].astype(BF16)

    def compute(rows):
        xs = xb_ref[0:rows, :]
        hg = jnp.dot(xs, w1g_b[...], preferred_element_type=F32) + b1g_ref[...]
        hl = jnp.dot(xs, w1l_b[...], preferred_element_type=F32) + b1l_ref[...]
        glu = jnp.minimum(hg, SWIGLU_LIMIT)
        lin = jnp.clip(hl, -SWIGLU_LIMIT, SWIGLU_LIMIT)
        act = (glu * jax.nn.sigmoid(SWIGLU_ALPHA * glu) * (lin + 1.0)).astype(BF16)
        col = min(512, d)
        for n in range(d // col):
            contrib = jnp.dot(act, w2_b[:, n * col:(n + 1) * col], preferred_element_type=F32)

            @pl.when(f == 0)
            def _():
                y_ref[0:rows, n * col:(n + 1) * col] = contrib + b2_ref[:, n * col:(n + 1) * col]

            @pl.when(f > 0)
            def _():
                y_ref[0:rows, n * col:(n + 1) * col] += contrib

    @pl.when(nv > SMALL_ROWS)
    def _():
        compute(CHUNK_ROWS)

    @pl.when((nv > 0) & (nv <= SMALL_ROWS))
    def _():
        compute(SMALL_ROWS)

    @pl.when(f == nf - 1)
    def _():
        sizes = []
        size = SUBLANES
        while size <= CHUNK_ROWS:
            sizes.append(size)
            size *= 2

        def piece(size):
            off = pl.multiple_of(nv - (nv % (2 * size)), SUBLANES)
            return pltpu.make_async_copy(y_ref.at[pl.ds(off, size), :], y_hbm.at[pl.ds(row0 + off, size), :],
                                         sem.at[1])

        for size in sizes:
            @pl.when((nv & size) != 0)
            def _():
                piece(size).start()
        for size in sizes:
            @pl.when((nv & size) != 0)
            def _():
                piece(size).wait()


def _expert_ffn(chunk_e, chunk_row, chunk_nv, x_buf, w1, b1, w2, b2, *, layer, tf):
    p, d = x_buf.shape
    n_experts, _, f2 = w1.shape[1:]
    d_ff = f2 // 2
    nf = d_ff // tf
    nc = chunk_e.shape[0]
    b1r = b1.reshape(b1.shape[0], n_experts, 1, f2)
    b2r = b2.reshape(b2.shape[0], n_experts, 1, d)

    def f_eff(c, f, cn):
        return jnp.where(cn[c] > 0, f, nf - 1)

    kern = functools.partial(_expert_kernel, nf=nf)
    return pl.pallas_call(
        kern,
        grid_spec=pltpu.PrefetchScalarGridSpec(
            num_scalar_prefetch=3,
            grid=(nc, nf),
            in_specs=[
                pl.BlockSpec(memory_space=pl.ANY),
                pl.BlockSpec((None, None, d, tf), lambda c, f, ce, cr, cn: (layer, ce[c], 0, f_eff(c, f, cn))),
                pl.BlockSpec((None, None, d, tf), lambda c, f, ce, cr, cn: (layer, ce[c], 0, nf + f_eff(c, f, cn))),
                pl.BlockSpec((None, None, 1, tf), lambda c, f, ce, cr, cn: (layer, ce[c], 0, f_eff(c, f, cn))),
                pl.BlockSpec((None, None, 1, tf), lambda c, f, ce, cr, cn: (layer, ce[c], 0, nf + f_eff(c, f, cn))),
                pl.BlockSpec((None, None, tf, d), lambda c, f, ce, cr, cn: (layer, ce[c], f_eff(c, f, cn), 0)),
                pl.BlockSpec((None, None, 1, d), lambda c, f, ce, cr, cn: (layer, ce[c], 0, 0)),
            ],
            out_specs=pl.BlockSpec(memory_space=pl.ANY),
            scratch_shapes=[
                pltpu.VMEM((CHUNK_ROWS, d), BF16),
                pltpu.VMEM((CHUNK_ROWS, d), F32),
                pltpu.VMEM((CHUNK_ROWS, d), F32),
                pltpu.VMEM((d, tf), BF16),
                pltpu.VMEM((d, tf), BF16),
                pltpu.VMEM((tf, d), BF16),
                pltpu.SemaphoreType.DMA((2,)),
            ],
        ),
        out_shape=jax.ShapeDtypeStruct((p, d), F32),
        input_output_aliases={3: 0},
        compiler_params=_cparams(("arbitrary", "arbitrary")),
        name="expert_ffn",
    )(chunk_e, chunk_row, chunk_nv, x_buf, w1, w1, b1r, b1r, w2, b2r)


def _combine_kernel(dest_ref, y_hbm, gate_ref, h_ref, g_ref, b_ref, out_ref, buf_ref, sem, *, tm, alpha):
    base = pl.program_id(0) * tm

    def row_copy(t, k):
        r = dest_ref[(base + t) * TOP_K + k]
        return pltpu.make_async_copy(y_hbm.at[pl.ds(r, 1), :], buf_ref.at[k, pl.ds(t, 1), :], sem)

    def issue(t, carry):
        for k in range(TOP_K):
            row_copy(t, k).start()
        return carry

    def drain(t, carry):
        for k in range(TOP_K):
            row_copy(t, k).wait()
        return carry

    lax.fori_loop(0, tm, issue, 0)
    lax.fori_loop(0, tm, drain, 0)

    gates = gate_ref[...]
    y = gates[:, 0:1] * buf_ref[0]
    for k in range(1, TOP_K):
        y = y + gates[:, k:k + 1] * buf_ref[k]
    out_ref[...] = _layer_norm_rows(alpha * h_ref[...] + y, g_ref[...], b_ref[...])


def _combine_ln(dest_flat, y_buf, gates, h2d, g, b, *, alpha, tm):
    t, d = h2d.shape
    kern = functools.partial(_combine_kernel, tm=tm, alpha=alpha)
    return pl.pallas_call(
        kern,
        grid_spec=pltpu.PrefetchScalarGridSpec(
            num_scalar_prefetch=1,
            grid=(t // tm,),
            in_specs=[
                pl.BlockSpec(memory_space=pl.ANY),
                pl.BlockSpec((tm, LANES), lambda i, dest: (i, 0)),
                pl.BlockSpec((tm, d), lambda i, dest: (i, 0)),
                pl.BlockSpec((1, d), lambda i, dest: (0, 0)),
                pl.BlockSpec((1, d), lambda i, dest: (0, 0)),
            ],
            out_specs=pl.BlockSpec((tm, d), lambda i, dest: (i, 0)),
            scratch_shapes=[pltpu.VMEM((TOP_K, tm, d), F32), pltpu.SemaphoreType.DMA(())],
        ),
        out_shape=jax.ShapeDtypeStruct((t, d), F32),
        compiler_params=_cparams(("arbitrary",)),
        name="combine_ln",
    )(dest_flat, y_buf, gates, h2d, g, b)


def _routing_tables(idx, rank, counts, *, n_chunks):
    padded = (counts + SUBLANES - 1) // SUBLANES * SUBLANES
    pad_end = jnp.cumsum(padded)
    pad_start = pad_end - padded
    dest = pad_start[idx] + rank
    nchunk = (padded + CHUNK_ROWS - 1) // CHUNK_ROWS
    chunk_end = jnp.cumsum(nchunk)
    chunk_start = chunk_end - nchunk
    total = chunk_end[-1]
    cids = jnp.arange(n_chunks, dtype=jnp.int32)
    clamped = jnp.minimum(cids, total - 1)
    ce = jnp.sum((chunk_end[None, :] <= clamped[:, None]).astype(jnp.int32), axis=1)
    local = clamped - chunk_start[ce]
    crow = (pad_start[ce] + local * CHUNK_ROWS).astype(jnp.int32)
    cnv = jnp.clip(padded[ce] - local * CHUNK_ROWS, 0, CHUNK_ROWS)
    cnv = jnp.where(cids < total, cnv, 0).astype(jnp.int32)
    return dest.reshape(-1).astype(jnp.int32), ce.astype(jnp.int32), crow, cnv


def _moe_ln(h2d, router_w, router_b, w1, b1, w2, b2, g, b, *, layer, alpha, tiles):
    t, d = h2d.shape
    n_experts = router_w.shape[1]
    w_pad = jnp.pad(router_w, ((0, 0), (0, LANES - n_experts)))
    b_pad = jnp.pad(router_b, (0, LANES - n_experts)).reshape(1, LANES)
    idx, gates, rank, counts = _router(h2d, w_pad, b_pad, n_experts=n_experts, tm=tiles["router_tm"])
    n_chunks = (t * TOP_K + n_experts * SUBLANES) // CHUNK_ROWS + n_experts
    dest_flat, ce, crow, cnv = _routing_tables(idx[:, :TOP_K], rank[:, :TOP_K], counts[0, :n_experts],
                                               n_chunks=n_chunks)
    p_rows = t * TOP_K + n_experts * SUBLANES + CHUNK_ROWS
    x_buf = _dispatch(dest_flat, h2d, jnp.zeros((p_rows, d), F32), tm=tiles["dispatch_tm"])
    y_buf = _expert_ffn(ce, crow, cnv, x_buf, w1, b1, w2, b2, layer=layer, tf=tiles["expert_tf"])
    return _combine_ln(dest_flat, y_buf, gates, h2d, g, b, alpha=alpha, tm=tiles["combine_tm"])


def _rope_tables(positions, q_scale):
    inv_freq = ROPE_THETA ** (-jnp.arange(0, HEAD_DIM, 2, dtype=F32) / HEAD_DIM)
    ang = positions.astype(F32).reshape(-1)[:, None] * inv_freq
    cos, sin = jnp.cos(ang), jnp.sin(ang)
    cos2 = jnp.concatenate([cos, cos], axis=-1)
    sin2 = jnp.concatenate([-sin, sin], axis=-1)
    k_tab = jnp.stack([cos2, sin2])
    return jnp.stack([k_tab * q_scale, k_tab])


def _tiles(t, s, d, d_ff):
    return {
        "inproj_tm": min(1024, t), "inproj_tn": min(1024, d),
        "attn_tq": min(256, s),
        "outproj_tm": min(256, t),
        "router_tm": min(256, t),
        "dispatch_tm": min(256, t),
        "combine_tm": min(128, t),
        "expert_tf": min(256, d_ff),
    }


def kernel(x, positions, da_w_in, da_lambda_q1, da_lambda_k1, da_lambda_q2, da_lambda_k2, da_subln_g, da_w_out,
           sb_w_in, sb_w_out, ln1_g, ln1_b, ln2_g, ln2_b, router_w, router_b, expert_w1, expert_b1, expert_w2,
           expert_b2):
    bsz, seq, d = x.shape
    t = bsz * seq
    depth = ln1_g.shape[0]
    d_ff = expert_w2.shape[2]
    alpha = (2 * depth) ** 0.25
    q_scale = HEAD_DIM ** -0.5
    tiles = _tiles(t, seq, d, d_ff)
    tab = _rope_tables(positions, q_scale)

    h = x.reshape(t, d)
    for i in range(depth):
        j = i // 2
        if i % 2 == 0:
            lambda_init = 0.8 - 0.6 * math.exp(-0.3 * i)
            proj = _in_projection(h, da_w_in[j].astype(BF16), tab, rope=True, d_model=d, q_scale=q_scale,
                                  tm=tiles["inproj_tm"], tn=tiles["inproj_tn"])
            lam_rows = jnp.stack([da_lambda_q1[j], da_lambda_k1[j], da_lambda_q2[j], da_lambda_k2[j]])
            o = _diff_attention(proj.reshape(bsz, seq, 3 * d), lam_rows, da_subln_g[j].reshape(1, DA_V_DIM),
                                lambda_init=lambda_init, tq=tiles["attn_tq"])
            w_out = da_w_out[j]
        else:
            proj = _in_projection(h, sb_w_in[j].astype(BF16), tab, rope=False, d_model=d, q_scale=q_scale,
                                  tm=tiles["inproj_tm"], tn=tiles["inproj_tn"])
            o = _stick_breaking_attention(proj.reshape(bsz, seq, 3 * d), tq=tiles["attn_tq"])
            w_out = sb_w_out[j]
        h = _out_projection_ln(o.reshape(t, d), w_out.astype(BF16), h, ln1_g[i].reshape(1, d),
                               ln1_b[i].reshape(1, d), alpha=alpha, tm=tiles["outproj_tm"])
        h = _moe_ln(h, router_w[i], router_b[i], expert_w1, expert_b1, expert_w2, expert_b2,
                    ln2_g[i].reshape(1, d), ln2_b[i].reshape(1, d), layer=i, alpha=alpha, tiles=tiles)
    return h.reshape(bsz, seq, d)
```

```python
import functools
import math

import jax
import jax.numpy as jnp
from jax import lax
from jax.experimental import pallas as pl
from jax.experimental.pallas import tpu as pltpu

F32 = jnp.float32
BF16 = jnp.bfloat16

HEAD_DIM = 128
DA_V_DIM = 2 * HEAD_DIM
TOP_K = 4
SWIGLU_ALPHA = 1.702
SWIGLU_LIMIT = 7.0
ROPE_THETA = 10000.0
LN_EPS = 1e-5

LANES = 128
SUBLANES = 8
VMEM_LIMIT_BYTES = 56 * 1024 * 1024

CHUNK_ROWS = 1152
SMALL_ROWS = 128


def _cparams(sem):
    return pltpu.CompilerParams(dimension_semantics=sem, vmem_limit_bytes=VMEM_LIMIT_BYTES)


def _layer_norm_rows(z, g, b):
    mu = jnp.mean(z, axis=-1, keepdims=True)
    zc = z - mu
    var = jnp.mean(zc * zc, axis=-1, keepdims=True)
    return zc * lax.rsqrt(var + LN_EPS) * g + b


def _inproj_kernel(x_ref, w_ref, tab_ref, o_ref, xb_ref, *, rope, qk_tiles, q_tiles, q_scale):
    j = pl.program_id(1)

    @pl.when(j == 0)
    def _():
        xb_ref[...] = x_ref[...].astype(BF16)

    acc = jnp.dot(xb_ref[...], w_ref[...], preferred_element_type=F32)
    tn = acc.shape[1]

    if rope:
        @pl.when(j < qk_tiles)
        def _():
            cos = tab_ref[0]
            sin = tab_ref[1]
            for c in range(tn // LANES):
                xc = acc[:, c * LANES:(c + 1) * LANES]
                rot = pltpu.roll(xc, LANES // 2, 1)
                o_ref[:, c * LANES:(c + 1) * LANES] = (xc * cos + rot * sin).astype(o_ref.dtype)

        @pl.when(j >= qk_tiles)
        def _():
            o_ref[...] = acc.astype(o_ref.dtype)
    else:
        @pl.when(j < q_tiles)
        def _():
            o_ref[...] = (acc * q_scale).astype(o_ref.dtype)

        @pl.when(j >= q_tiles)
        def _():
            o_ref[...] = acc.astype(o_ref.dtype)


def _in_projection(x2d, w_bf16, tab, *, rope, d_model, q_scale, tm, tn):
    m, k = x2d.shape
    n = w_bf16.shape[1]
    q_tiles = d_model // tn
    qk_tiles = 2 * q_tiles
    kern = functools.partial(_inproj_kernel, rope=rope, qk_tiles=qk_tiles, q_tiles=q_tiles, q_scale=q_scale)
    return pl.pallas_call(
        kern,
        grid=(m // tm, n // tn),
        in_specs=[
            pl.BlockSpec((tm, k), lambda i, j: (i, 0)),
            pl.BlockSpec((k, tn), lambda i, j: (0, j)),
            pl.BlockSpec((None, 2, tm, LANES), lambda i, j: (jnp.minimum(j // q_tiles, 1), 0, i, 0)),
        ],
        out_specs=pl.BlockSpec((tm, tn), lambda i, j: (i, j)),
        out_shape=jax.ShapeDtypeStruct((m, n), BF16),
        scratch_shapes=[pltpu.VMEM((tm, k), BF16)],
        compiler_params=_cparams(("parallel", "arbitrary")),
        name="in_projection",
    )(x2d, w_bf16, tab)


def _transpose_bf16(x):
    return x.astype(F32).T.astype(BF16)


def _store_v_transposed(v_ref, vt_ref, tk):
    for n in range(vt_ref.shape[0]):
        vt_ref[n] = _transpose_bf16(v_ref[n * tk:(n + 1) * tk, :])


def _da_kernel(q_ref, k_ref, v_ref, lam_ref, g_ref, o_ref, vt_ref, acc_ref, m_ref, l_ref, *, tq, tk, lambda_init):
    i = pl.program_id(2)
    diag_tiles = tq // tk

    @pl.when(i == 0)
    def _():
        _store_v_transposed(v_ref, vt_ref, tk)

    m_ref[...] = jnp.full(m_ref.shape, -jnp.inf, F32)
    l_ref[...] = jnp.zeros(l_ref.shape, F32)
    acc_ref[...] = jnp.zeros(acc_ref.shape, F32)
    q_t = _transpose_bf16(q_ref[...])
    key = lax.broadcasted_iota(jnp.int32, (tk, tq), 0)
    qry = lax.broadcasted_iota(jnp.int32, (tk, tq), 1)

    def step(j, shift):
        off = pl.multiple_of(j * tk, tk)
        ks = k_ref[pl.ds(off, tk), :]
        v_t = vt_ref[j]
        comps = range(2)
        s = [jnp.dot(ks[:, c * HEAD_DIM:(c + 1) * HEAD_DIM], q_t[c * HEAD_DIM:(c + 1) * HEAD_DIM, :],
                     preferred_element_type=F32) for c in comps]
        if shift is not None:
            visible = key + shift <= qry
            s = [jnp.where(visible, sc, -jnp.inf) for sc in s]
        m_old = [m_ref[c] for c in comps]
        m_new = [jnp.maximum(m_old[c], jnp.max(s[c], axis=0, keepdims=True)) for c in comps]
        alpha = [jnp.exp2(m_old[c] - m_new[c]) for c in comps]
        p = [jnp.exp2(s[c] - m_new[c]) for c in comps]
        for c in comps:
            l_ref[c] = alpha[c] * l_ref[c] + jnp.sum(p[c], axis=0, keepdims=True)
            m_ref[c] = m_new[c]
        pv = [jnp.dot(v_t, p[c].astype(BF16), preferred_element_type=F32) for c in comps]
        for c in comps:
            acc_ref[c] = alpha[c] * acc_ref[c] + pv[c]

    def body(j, carry):
        step(j, None)
        return carry

    lax.fori_loop(0, i * diag_tiles, body, 0)
    for dj in range(diag_tiles):
        step(i * diag_tiles + dj, dj * tk)

    lam_rows = lam_ref[...]
    lam = (jnp.exp(jnp.sum(lam_rows[0:1] * lam_rows[1:2], axis=-1, keepdims=True))
           - jnp.exp(jnp.sum(lam_rows[2:3] * lam_rows[3:4], axis=-1, keepdims=True)) + lambda_init)
    o_t = acc_ref[0] / l_ref[0] - lam * (acc_ref[1] / l_ref[1])
    o = o_t.T
    o = o * lax.rsqrt(jnp.mean(o * o, axis=-1, keepdims=True) + LN_EPS) * g_ref[...]
    o_ref[...] = (o * (1.0 - lambda_init)).astype(o_ref.dtype)


def _diff_attention(proj, lam_rows, subln_g, *, lambda_init, tq, tk):
    b, s, n3 = proj.shape
    d_model = n3 // 3
    heads = d_model // DA_V_DIM
    kern = functools.partial(_da_kernel, tq=tq, tk=tk, lambda_init=lambda_init)
    return pl.pallas_call(
        kern,
        grid=(b, heads, s // tq),
        in_specs=[
            pl.BlockSpec((None, tq, DA_V_DIM), lambda bi, h, i: (bi, i, h)),
            pl.BlockSpec((None, s, DA_V_DIM), lambda bi, h, i: (bi, 0, heads + h)),
            pl.BlockSpec((None, s, DA_V_DIM), lambda bi, h, i: (bi, 0, 2 * heads + h)),
            pl.BlockSpec((4, HEAD_DIM), lambda bi, h, i: (0, 0)),
            pl.BlockSpec((1, DA_V_DIM), lambda bi, h, i: (0, 0)),
        ],
        out_specs=pl.BlockSpec((None, tq, DA_V_DIM), lambda bi, h, i: (bi, i, h)),
        out_shape=jax.ShapeDtypeStruct((b, s, d_model), BF16),
        scratch_shapes=[pltpu.VMEM((s // tk, DA_V_DIM, tk), BF16), pltpu.VMEM((2, DA_V_DIM, tq), F32),
                        pltpu.VMEM((2, 1, tq), F32), pltpu.VMEM((2, 1, tq), F32)],
        compiler_params=_cparams(("parallel", "parallel", "arbitrary")),
        name="diff_attention",
    )(proj, proj, proj, lam_rows, subln_g)


def _sb_kernel(q_ref, k_ref, v_ref, o_ref, vt_ref, acc_ref, run_ref, *, tq, tk, hp):
    i = pl.program_id(2)
    diag_tiles = tq // tk
    heads = range(hp)

    def head(x, h, axis):
        return x[:, h * HEAD_DIM:(h + 1) * HEAD_DIM] if axis == 1 else x[h * HEAD_DIM:(h + 1) * HEAD_DIM, :]

    @pl.when(i == 0)
    def _():
        _store_v_transposed(v_ref, vt_ref, tk)

    acc_ref[...] = jnp.zeros(acc_ref.shape, F32)
    run_ref[...] = jnp.zeros(run_ref.shape, F32)
    q_t = _transpose_bf16(q_ref[...])
    key = lax.broadcasted_iota(jnp.int32, (tk, tq), 0)
    qry = lax.broadcasted_iota(jnp.int32, (tk, tq), 1)
    k_a = lax.broadcasted_iota(jnp.int32, (tk, tk), 0)
    k_b = lax.broadcasted_iota(jnp.int32, (tk, tk), 1)
    later = (k_b > k_a).astype(BF16)

    def step(j, shift):
        off = pl.multiple_of(j * tk, tk)
        ks = k_ref[pl.ds(off, tk), :]
        v_t = vt_ref[j]
        z = [jnp.dot(head(ks, h, 1), head(q_t, h, 0), preferred_element_type=F32) for h in heads]
        log_beta = [jnp.minimum(zh, 0.0) - jnp.log2(1.0 + jnp.exp2(-jnp.abs(zh))) for zh in z]
        log_1m = [log_beta[h] - z[h] for h in heads]
        if shift is not None:
            strict = key + shift < qry
            log_1m = [jnp.where(strict, x, 0.0) for x in log_1m]
        hi = [x.astype(BF16) for x in log_1m]
        lo = [(log_1m[h] - hi[h].astype(F32)).astype(BF16) for h in heads]
        after = [jnp.dot(later, hi[h], preferred_element_type=F32)
                 + jnp.dot(later, lo[h], preferred_element_type=F32) for h in heads]
        run = [run_ref[h] for h in heads]
        a = [jnp.exp2(log_beta[h] + after[h] + run[h]) for h in heads]
        if shift is not None:
            a = [jnp.where(strict, x, 0.0) for x in a]
        for h in heads:
            acc_ref[h] += jnp.dot(head(v_t, h, 0), a[h].astype(BF16), preferred_element_type=F32)
            run_ref[h] = run[h] + after[h][0:1, :] + log_1m[h][0:1, :]

    for dj in reversed(range(diag_tiles)):
        step(i * diag_tiles + dj, dj * tk)

    def body(jj, carry):
        step(i * diag_tiles - 1 - jj, None)
        return carry

    lax.fori_loop(0, i * diag_tiles, body, 0)
    for h in heads:
        o_ref[:, h * HEAD_DIM:(h + 1) * HEAD_DIM] = acc_ref[h].T.astype(o_ref.dtype)


def _stick_breaking_attention(proj, *, tq, tk, hp):
    b, s, n3 = proj.shape
    d_model = n3 // 3
    heads = d_model // HEAD_DIM
    groups = heads // hp
    width = hp * HEAD_DIM
    kern = functools.partial(_sb_kernel, tq=tq, tk=tk, hp=hp)
    return pl.pallas_call(
        kern,
        grid=(b, groups, s // tq),
        in_specs=[
            pl.BlockSpec((None, tq, width), lambda bi, h, i: (bi, i, h)),
            pl.BlockSpec((None, s, width), lambda bi, h, i: (bi, 0, groups + h)),
            pl.BlockSpec((None, s, width), lambda bi, h, i: (bi, 0, 2 * groups + h)),
        ],
        out_specs=pl.BlockSpec((None, tq, width), lambda bi, h, i: (bi, i, h)),
        out_shape=jax.ShapeDtypeStruct((b, s, d_model), BF16),
        scratch_shapes=[pltpu.VMEM((s // tk, width, tk), BF16), pltpu.VMEM((hp, HEAD_DIM, tq), F32),
                        pltpu.VMEM((hp, 1, tq), F32)],
        compiler_params=_cparams(("parallel", "parallel", "arbitrary")),
        name="stick_breaking_attention",
    )(proj, proj, proj)


def _outproj_ln_kernel(o_ref, w_ref, h_ref, g_ref, b_ref, out_ref, *, alpha):
    mix = jnp.dot(o_ref[...], w_ref[...], preferred_element_type=F32)
    out_ref[...] = _layer_norm_rows(alpha * h_ref[...] + mix, g_ref[...], b_ref[...])


def _out_projection_ln(o2d, w_bf16, h2d, g, b, *, alpha, tm):
    m, d = h2d.shape
    kern = functools.partial(_outproj_ln_kernel, alpha=alpha)
    return pl.pallas_call(
        kern,
        grid=(m // tm,),
        in_specs=[
            pl.BlockSpec((tm, d), lambda i: (i, 0)),
            pl.BlockSpec((d, d), lambda i: (0, 0)),
            pl.BlockSpec((tm, d), lambda i: (i, 0)),
            pl.BlockSpec((1, d), lambda i: (0, 0)),
            pl.BlockSpec((1, d), lambda i: (0, 0)),
        ],
        out_specs=pl.BlockSpec((tm, d), lambda i: (i, 0)),
        out_shape=jax.ShapeDtypeStruct((m, d), F32),
        compiler_params=_cparams(("parallel",)),
        name="out_projection_ln",
    )(o2d, w_bf16, h2d, g, b)


def _router_kernel(h_ref, w_ref, b_ref, idx_ref, gate_ref, rank_ref, cnt_ref, carry_ref, *, n_experts):
    step = pl.program_id(0)

    @pl.when(step == 0)
    def _():
        carry_ref[...] = jnp.zeros(carry_ref.shape, F32)

    tm = h_ref.shape[0]
    logits = jnp.dot(h_ref[...], w_ref[...], preferred_element_type=F32,
                     precision=lax.Precision.HIGHEST) + b_ref[...]
    lane = lax.broadcasted_iota(jnp.int32, (tm, LANES), 1)
    work = jnp.where(lane < n_experts, logits, -jnp.inf)

    vals, idxs = [], []
    onehot = jnp.zeros((tm, LANES), F32)
    for _ in range(TOP_K):
        mx = jnp.max(work, axis=-1, keepdims=True)
        ix = jnp.min(jnp.where(work == mx, lane, LANES), axis=-1, keepdims=True)
        sel = lane == ix
        vals.append(mx)
        idxs.append(ix)
        onehot = jnp.where(sel, 1.0, onehot)
        work = jnp.where(sel, -jnp.inf, work)

    exps = [jnp.exp(v - vals[0]) for v in vals]
    denom = exps[0]
    for e in exps[1:]:
        denom = denom + e

    r_i = lax.broadcasted_iota(jnp.int32, (tm, tm), 0)
    c_i = lax.broadcasted_iota(jnp.int32, (tm, tm), 1)
    lower = (c_i < r_i).astype(BF16)
    earlier = jnp.dot(lower, onehot.astype(BF16), preferred_element_type=F32) + carry_ref[...]

    idx_out = jnp.zeros((tm, LANES), jnp.int32)
    gate_out = jnp.zeros((tm, LANES), F32)
    rank_out = jnp.zeros((tm, LANES), jnp.int32)
    for k in range(TOP_K):
        rk = jnp.sum(jnp.where(lane == idxs[k], earlier, 0.0), axis=-1, keepdims=True)
        idx_out = jnp.where(lane == k, idxs[k], idx_out)
        gate_out = jnp.where(lane == k, exps[k] / denom, gate_out)
        rank_out = jnp.where(lane == k, rk.astype(jnp.int32), rank_out)
    idx_ref[...] = idx_out
    gate_ref[...] = gate_out
    rank_ref[...] = rank_out

    total = carry_ref[...] + jnp.sum(onehot, axis=0, keepdims=True)
    carry_ref[...] = total
    cnt_ref[...] = total.astype(jnp.int32)


def _router(h2d, w_pad, b_pad, *, n_experts, tm):
    t, d = h2d.shape
    kern = functools.partial(_router_kernel, n_experts=n_experts)
    tok_spec = pl.BlockSpec((tm, LANES), lambda i: (i, 0))
    return pl.pallas_call(
        kern,
        grid=(t // tm,),
        in_specs=[
            pl.BlockSpec((tm, d), lambda i: (i, 0)),
            pl.BlockSpec((d, LANES), lambda i: (0, 0)),
            pl.BlockSpec((1, LANES), lambda i: (0, 0)),
        ],
        out_specs=[tok_spec, tok_spec, tok_spec, pl.BlockSpec((1, LANES), lambda i: (0, 0))],
        out_shape=[jax.ShapeDtypeStruct((t, LANES), jnp.int32), jax.ShapeDtypeStruct((t, LANES), F32),
                   jax.ShapeDtypeStruct((t, LANES), jnp.int32), jax.ShapeDtypeStruct((1, LANES), jnp.int32)],
        scratch_shapes=[pltpu.VMEM((1, LANES), F32)],
        compiler_params=_cparams(("arbitrary",)),
        name="router",
    )(h2d, w_pad, b_pad)


def _dispatch_kernel(dest_ref, h_ref, xin_hbm, x_hbm, sem, *, tm):
    del xin_hbm
    base = pl.program_id(0) * tm

    def row_copy(t, k):
        r = dest_ref[(base + t) * TOP_K + k]
        return pltpu.make_async_copy(h_ref.at[pl.ds(t, 1), :], x_hbm.at[pl.ds(r, 1), :], sem)

    def issue(t, carry):
        for k in range(TOP_K):
            row_copy(t, k).start()
        return carry

    def drain(t, carry):
        for k in range(TOP_K):
            row_copy(t, k).wait()
        return carry

    lax.fori_loop(0, tm, issue, 0)
    lax.fori_loop(0, tm, drain, 0)


def _dispatch(dest_flat, h2d, x_zero, *, tm):
    t, d = h2d.shape
    kern = functools.partial(_dispatch_kernel, tm=tm)
    return pl.pallas_call(
        kern,
        grid_spec=pltpu.PrefetchScalarGridSpec(
            num_scalar_prefetch=1,
            grid=(t // tm,),
            in_specs=[pl.BlockSpec((tm, d), lambda i, dest: (i, 0)), pl.BlockSpec(memory_space=pl.ANY)],
            out_specs=pl.BlockSpec(memory_space=pl.ANY),
            scratch_shapes=[pltpu.SemaphoreType.DMA(())],
        ),
        out_shape=jax.ShapeDtypeStruct(x_zero.shape, x_zero.dtype),
        input_output_aliases={2: 0},
        compiler_params=_cparams(("arbitrary",)),
        name="dispatch",
    )(dest_flat, h2d, x_zero)


def _expert_kernel(ce_ref, cr_ref, cn_ref, x_hbm, w1g_ref, w1l_ref, b1g_ref, b1l_ref, w2_ref, b2_ref, y_hbm,
                   xb_ref, y_ref, stage_ref, w1g_b, w1l_b, w2_b, sem, *, nf):
    del ce_ref
    c = pl.program_id(0)
    f = pl.program_id(1)
    nv = cn_ref[c]
    row0 = pl.multiple_of(cr_ref[c], SUBLANES)
    d = y_ref.shape[1]

    @pl.when((f == 0) & (nv > 0))
    def _():
        cp = pltpu.make_async_copy(x_hbm.at[pl.ds(row0, CHUNK_ROWS), :], stage_ref, sem.at[0])
        cp.start()
        cp.wait()
        xb_ref[...] = stage_ref[...].astype(BF16)

    @pl.when((f == 0) & (nv > 0))
    def _():
        y_ref[...] = jnp.broadcast_to(b2_ref[...], y_ref.shape)

    def compute(rows):
        xs = xb_ref[0:rows, :]
        w1g_b[...] = w1g_ref[...].astype(BF16)
        hg = jnp.dot(xs, w1g_b[...], preferred_element_type=F32) + b1g_ref[...]
        w1l_b[...] = w1l_ref[...].astype(BF16)
        hl = jnp.dot(xs, w1l_b[...], preferred_element_type=F32) + b1l_ref[...]
        glu = jnp.minimum(hg, SWIGLU_LIMIT)
        gate = glu * jax.nn.sigmoid(SWIGLU_ALPHA * glu)
        w2_b[...] = w2_ref[...].astype(BF16)
        lin = jnp.clip(hl, -SWIGLU_LIMIT, SWIGLU_LIMIT)
        act = (gate * (lin + 1.0)).astype(BF16)
        col = min(512, d)
        for n in range(d // col):
            y_ref[0:rows, n * col:(n + 1) * col] += jnp.dot(act, w2_b[:, n * col:(n + 1) * col],
                                                            preferred_element_type=F32)

    @pl.when(nv > SMALL_ROWS)
    def _():
        compute(CHUNK_ROWS)

    @pl.when((nv > 0) & (nv <= SMALL_ROWS))
    def _():
        compute(SMALL_ROWS)

    @pl.when(f == nf - 1)
    def _():
        sizes = []
        size = SUBLANES
        while size <= CHUNK_ROWS:
            sizes.append(size)
            size *= 2

        def piece(size):
            off = pl.multiple_of(nv - (nv % (2 * size)), SUBLANES)
            return pltpu.make_async_copy(y_ref.at[pl.ds(off, size), :], y_hbm.at[pl.ds(row0 + off, size), :],
                                         sem.at[1])

        for size in sizes:
            @pl.when((nv & size) != 0)
            def _():
                piece(size).start()
        for size in sizes:
            @pl.when((nv & size) != 0)
            def _():
                piece(size).wait()


def _expert_ffn(chunk_e, chunk_row, chunk_nv, x_buf, w1, b1, w2, b2, *, layer, tf):
    p, d = x_buf.shape
    n_experts, _, f2 = w1.shape[1:]
    d_ff = f2 // 2
    nf = d_ff // tf
    nc = chunk_e.shape[0]
    b1r = b1.reshape(b1.shape[0], n_experts, 1, f2)
    b2r = b2.reshape(b2.shape[0], n_experts, 1, d)

    def f_eff(c, f, cn):
        return jnp.where(cn[c] > 0, f, nf - 1)

    kern = functools.partial(_expert_kernel, nf=nf)
    return pl.pallas_call(
        kern,
        grid_spec=pltpu.PrefetchScalarGridSpec(
            num_scalar_prefetch=3,
            grid=(nc, nf),
            in_specs=[
                pl.BlockSpec(memory_space=pl.ANY),
                pl.BlockSpec((None, None, d, tf), lambda c, f, ce, cr, cn: (layer, ce[c], 0, f_eff(c, f, cn))),
                pl.BlockSpec((None, None, d, tf), lambda c, f, ce, cr, cn: (layer, ce[c], 0, nf + f_eff(c, f, cn))),
                pl.BlockSpec((None, None, 1, tf), lambda c, f, ce, cr, cn: (layer, ce[c], 0, f_eff(c, f, cn))),
                pl.BlockSpec((None, None, 1, tf), lambda c, f, ce, cr, cn: (layer, ce[c], 0, nf + f_eff(c, f, cn))),
                pl.BlockSpec((None, None, tf, d), lambda c, f, ce, cr, cn: (layer, ce[c], f_eff(c, f, cn), 0)),
                pl.BlockSpec((None, None, 1, d), lambda c, f, ce, cr, cn: (layer, ce[c], 0, 0)),
            ],
            out_specs=pl.BlockSpec(memory_space=pl.ANY),
            scratch_shapes=[
                pltpu.VMEM((CHUNK_ROWS, d), BF16),
                pltpu.VMEM((CHUNK_ROWS, d), F32),
                pltpu.VMEM((CHUNK_ROWS, d), F32),
                pltpu.VMEM((d, tf), BF16),
                pltpu.VMEM((d, tf), BF16),
                pltpu.VMEM((tf, d), BF16),
                pltpu.SemaphoreType.DMA((2,)),
            ],
        ),
        out_shape=jax.ShapeDtypeStruct((p, d), F32),
        input_output_aliases={3: 0},
        compiler_params=_cparams(("arbitrary", "arbitrary")),
        name="expert_ffn",
    )(chunk_e, chunk_row, chunk_nv, x_buf, w1, w1, b1r, b1r, w2, b2r)


def _combine_kernel(dest_ref, y_hbm, gate_ref, h_ref, g_ref, b_ref, out_ref, buf_ref, sem, *, tm, alpha):
    base = pl.program_id(0) * tm

    def row_copy(t, k):
        r = dest_ref[(base + t) * TOP_K + k]
        return pltpu.make_async_copy(y_hbm.at[pl.ds(r, 1), :], buf_ref.at[k, pl.ds(t, 1), :], sem)

    def issue(t, carry):
        for k in range(TOP_K):
            row_copy(t, k).start()
        return carry

    def drain(t, carry):
        for k in range(TOP_K):
            row_copy(t, k).wait()
        return carry

    lax.fori_loop(0, tm, issue, 0)
    lax.fori_loop(0, tm, drain, 0)

    gates = gate_ref[...]
    y = gates[:, 0:1] * buf_ref[0]
    for k in range(1, TOP_K):
        y = y + gates[:, k:k + 1] * buf_ref[k]
    out_ref[...] = _layer_norm_rows(alpha * h_ref[...] + y, g_ref[...], b_ref[...])


def _combine_ln(dest_flat, y_buf, gates, h2d, g, b, *, alpha, tm):
    t, d = h2d.shape
    kern = functools.partial(_combine_kernel, tm=tm, alpha=alpha)
    return pl.pallas_call(
        kern,
        grid_spec=pltpu.PrefetchScalarGridSpec(
            num_scalar_prefetch=1,
            grid=(t // tm,),
            in_specs=[
                pl.BlockSpec(memory_space=pl.ANY),
                pl.BlockSpec((tm, LANES), lambda i, dest: (i, 0)),
                pl.BlockSpec((tm, d), lambda i, dest: (i, 0)),
                pl.BlockSpec((1, d), lambda i, dest: (0, 0)),
                pl.BlockSpec((1, d), lambda i, dest: (0, 0)),
            ],
            out_specs=pl.BlockSpec((tm, d), lambda i, dest: (i, 0)),
            scratch_shapes=[pltpu.VMEM((TOP_K, tm, d), F32), pltpu.SemaphoreType.DMA(())],
        ),
        out_shape=jax.ShapeDtypeStruct((t, d), F32),
        compiler_params=_cparams(("arbitrary",)),
        name="combine_ln",
    )(dest_flat, y_buf, gates, h2d, g, b)


def _routing_tables(idx, rank, counts, *, n_chunks):
    padded = (counts + SUBLANES - 1) // SUBLANES * SUBLANES
    pad_end = jnp.cumsum(padded)
    pad_start = pad_end - padded
    dest = pad_start[idx] + rank
    nchunk = (padded + CHUNK_ROWS - 1) // CHUNK_ROWS
    chunk_end = jnp.cumsum(nchunk)
    chunk_start = chunk_end - nchunk
    total = chunk_end[-1]
    cids = jnp.arange(n_chunks, dtype=jnp.int32)
    clamped = jnp.minimum(cids, total - 1)
    ce = jnp.sum((chunk_end[None, :] <= clamped[:, None]).astype(jnp.int32), axis=1)
    local = clamped - chunk_start[ce]
    crow = (pad_start[ce] + local * CHUNK_ROWS).astype(jnp.int32)
    cnv = jnp.clip(padded[ce] - local * CHUNK_ROWS, 0, CHUNK_ROWS)
    cnv = jnp.where(cids < total, cnv, 0).astype(jnp.int32)
    return dest.reshape(-1).astype(jnp.int32), ce.astype(jnp.int32), crow, cnv


def _moe_ln(h2d, router_w, router_b, w1, b1, w2, b2, g, b, *, layer, alpha, tiles):
    t, d = h2d.shape
    n_experts = router_w.shape[1]
    w_pad = jnp.pad(router_w, ((0, 0), (0, LANES - n_experts)))
    b_pad = jnp.pad(router_b, (0, LANES - n_experts)).reshape(1, LANES)
    idx, gates, rank, counts = _router(h2d, w_pad, b_pad, n_experts=n_experts, tm=tiles["router_tm"])
    n_chunks = (t * TOP_K + n_experts * SUBLANES) // CHUNK_ROWS + n_experts
    dest_flat, ce, crow, cnv = _routing_tables(idx[:, :TOP_K], rank[:, :TOP_K], counts[0, :n_experts],
                                               n_chunks=n_chunks)
    p_rows = t * TOP_K + n_experts * SUBLANES + CHUNK_ROWS
    x_buf = _dispatch(dest_flat, h2d, jnp.zeros((p_rows, d), F32), tm=tiles["dispatch_tm"])
    y_buf = _expert_ffn(ce, crow, cnv, x_buf, w1, b1, w2, b2, layer=layer, tf=tiles["expert_tf"])
    return _combine_ln(dest_flat, y_buf, gates, h2d, g, b, alpha=alpha, tm=tiles["combine_tm"])


def _rope_tables(positions, q_scale):
    inv_freq = ROPE_THETA ** (-jnp.arange(0, HEAD_DIM, 2, dtype=F32) / HEAD_DIM)
    ang = positions.astype(F32).reshape(-1)[:, None] * inv_freq
    cos, sin = jnp.cos(ang), jnp.sin(ang)
    cos2 = jnp.concatenate([cos, cos], axis=-1)
    sin2 = jnp.concatenate([-sin, sin], axis=-1)
    k_tab = jnp.stack([cos2, sin2])
    return jnp.stack([k_tab * q_scale, k_tab])


def _tiles(t, s, d, d_ff):
    return {
        "inproj_tm": min(1024, t), "inproj_tn": min(1024, d),
        "da_tq": min(512, s), "da_tk": min(512, s),
        "sb_tq": min(512, s), "sb_tk": min(256, s), "sb_heads_per_step": 2,
        "outproj_tm": min(256, t),
        "router_tm": min(256, t),
        "dispatch_tm": min(256, t),
        "combine_tm": min(128, t),
        "expert_tf": min(256, d_ff),
    }


def kernel(x, positions, da_w_in, da_lambda_q1, da_lambda_k1, da_lambda_q2, da_lambda_k2, da_subln_g, da_w_out,
           sb_w_in, sb_w_out, ln1_g, ln1_b, ln2_g, ln2_b, router_w, router_b, expert_w1, expert_b1, expert_w2,
           expert_b2):
    bsz, seq, d = x.shape
    t = bsz * seq
    depth = ln1_g.shape[0]
    d_ff = expert_w2.shape[2]
    alpha = (2 * depth) ** 0.25
    q_scale = HEAD_DIM ** -0.5 * math.log2(math.e)
    tiles = _tiles(t, seq, d, d_ff)
    tab = _rope_tables(positions, q_scale)

    h = x.reshape(t, d)
    for i in range(depth):
        j = i // 2
        if i % 2 == 0:
            lambda_init = 0.8 - 0.6 * math.exp(-0.3 * i)
            proj = _in_projection(h, da_w_in[j].astype(BF16), tab, rope=True, d_model=d, q_scale=q_scale,
                                  tm=tiles["inproj_tm"], tn=tiles["inproj_tn"])
            lam_rows = jnp.stack([da_lambda_q1[j], da_lambda_k1[j], da_lambda_q2[j], da_lambda_k2[j]])
            o = _diff_attention(proj.reshape(bsz, seq, 3 * d), lam_rows, da_subln_g[j].reshape(1, DA_V_DIM),
                                lambda_init=lambda_init, tq=tiles["da_tq"], tk=tiles["da_tk"])
            w_out = da_w_out[j]
        else:
            proj = _in_projection(h, sb_w_in[j].astype(BF16), tab, rope=False, d_model=d, q_scale=q_scale,
                                  tm=tiles["inproj_tm"], tn=tiles["inproj_tn"])
            o = _stick_breaking_attention(proj.reshape(bsz, seq, 3 * d), tq=tiles["sb_tq"], tk=tiles["sb_tk"],
                                          hp=tiles["sb_heads_per_step"])
            w_out = sb_w_out[j]
        h = _out_projection_ln(o.reshape(t, d), w_out.astype(BF16), h, ln1_g[i].reshape(1, d),
                               ln1_b[i].reshape(1, d), alpha=alpha, tm=tiles["outproj_tm"])
        h = _moe_ln(h, router_w[i], router_b[i], expert_w1, expert_b1, expert_w2, expert_b2,
                    ln2_g[i].reshape(1, d), ln2_b[i].reshape(1, d), layer=i, alpha=alpha, tiles=tiles)
    return h.reshape(bsz, seq, d)
```

```python
import functools
import math

import jax
import jax.numpy as jnp
from jax import lax
from jax.experimental import pallas as pl
from jax.experimental.pallas import tpu as pltpu

F32 = jnp.float32
BF16 = jnp.bfloat16

HEAD_DIM = 128
DA_V_DIM = 2 * HEAD_DIM
TOP_K = 4
SWIGLU_ALPHA = 1.702
SWIGLU_LIMIT = 7.0
ROPE_THETA = 10000.0
LN_EPS = 1e-5

LANES = 128
SUBLANES = 8
VMEM_LIMIT_BYTES = 56 * 1024 * 1024

CHUNK_ROWS = 1152
SMALL_ROWS = 128
PIPE_ROWS = 384
DMA_ISSUE_UNROLL = 8


def _cparams(sem):
    return pltpu.CompilerParams(dimension_semantics=sem, vmem_limit_bytes=VMEM_LIMIT_BYTES)


def _layer_norm_rows(z, g, b):
    mu = jnp.mean(z, axis=-1, keepdims=True)
    zc = z - mu
    var = jnp.mean(zc * zc, axis=-1, keepdims=True)
    return zc * lax.rsqrt(var + LN_EPS) * g + b


def _inproj_kernel(x_ref, w_ref, tab_ref, o_ref, xb_ref, *, rope, qk_tiles, q_tiles, q_scale):
    j = pl.program_id(1)

    @pl.when(j == 0)
    def _():
        xb_ref[...] = x_ref[...].astype(BF16)

    acc = jnp.dot(xb_ref[...], w_ref[...], preferred_element_type=F32)
    tn = acc.shape[1]

    if rope:
        @pl.when(j < qk_tiles)
        def _():
            cos = tab_ref[0]
            sin = tab_ref[1]
            for c in range(tn // LANES):
                xc = acc[:, c * LANES:(c + 1) * LANES]
                rot = pltpu.roll(xc, LANES // 2, 1)
                o_ref[:, c * LANES:(c + 1) * LANES] = (xc * cos + rot * sin).astype(o_ref.dtype)

        @pl.when(j >= qk_tiles)
        def _():
            o_ref[...] = acc.astype(o_ref.dtype)
    else:
        @pl.when(j < q_tiles)
        def _():
            o_ref[...] = (acc * q_scale).astype(o_ref.dtype)

        @pl.when(j >= q_tiles)
        def _():
            o_ref[...] = acc.astype(o_ref.dtype)


def _in_projection(x2d, w_bf16, tab, *, rope, d_model, q_scale, tm, tn):
    m, k = x2d.shape
    n = w_bf16.shape[1]
    q_tiles = d_model // tn
    qk_tiles = 2 * q_tiles
    kern = functools.partial(_inproj_kernel, rope=rope, qk_tiles=qk_tiles, q_tiles=q_tiles, q_scale=q_scale)
    return pl.pallas_call(
        kern,
        grid=(m // tm, n // tn),
        in_specs=[
            pl.BlockSpec((tm, k), lambda i, j: (i, 0)),
            pl.BlockSpec((k, tn), lambda i, j: (0, j)),
            pl.BlockSpec((None, 2, tm, LANES), lambda i, j: (jnp.minimum(j // q_tiles, 1), 0, i, 0)),
        ],
        out_specs=pl.BlockSpec((tm, tn), lambda i, j: (i, j)),
        out_shape=jax.ShapeDtypeStruct((m, n), BF16),
        scratch_shapes=[pltpu.VMEM((tm, k), BF16)],
        compiler_params=_cparams(("parallel", "arbitrary")),
        name="in_projection",
    )(x2d, w_bf16, tab)


def _transpose_bf16(x):
    return x.astype(F32).T.astype(BF16)


def _store_v_transposed(v_ref, vt_ref, tk):
    for n in range(vt_ref.shape[0]):
        vt_ref[n] = _transpose_bf16(v_ref[n * tk:(n + 1) * tk, :])


def _da_kernel(q_ref, k_ref, v_ref, lam_ref, g_ref, o_ref, vt_ref, acc_ref, m_ref, l_ref, *, tq, tk, lambda_init):
    i = pl.program_id(2)
    diag_tiles = tq // tk

    @pl.when(i == 0)
    def _():
        _store_v_transposed(v_ref, vt_ref, tk)

    m_ref[...] = jnp.full(m_ref.shape, -jnp.inf, F32)
    l_ref[...] = jnp.zeros(l_ref.shape, F32)
    acc_ref[...] = jnp.zeros(acc_ref.shape, F32)
    q_t = _transpose_bf16(q_ref[...])
    key = lax.broadcasted_iota(jnp.int32, (tk, tq), 0)
    qry = lax.broadcasted_iota(jnp.int32, (tk, tq), 1)

    def step(j, shift):
        off = pl.multiple_of(j * tk, tk)
        ks = k_ref[pl.ds(off, tk), :]
        v_t = vt_ref[j]
        comps = range(2)
        s = [jnp.dot(ks[:, c * HEAD_DIM:(c + 1) * HEAD_DIM], q_t[c * HEAD_DIM:(c + 1) * HEAD_DIM, :],
                     preferred_element_type=F32) for c in comps]
        if shift is not None:
            visible = key + shift <= qry
            s = [jnp.where(visible, sc, -jnp.inf) for sc in s]
        m_old = [m_ref[c] for c in comps]
        m_new = [jnp.maximum(m_old[c], jnp.max(s[c], axis=0, keepdims=True)) for c in comps]
        alpha = [jnp.exp2(m_old[c] - m_new[c]) for c in comps]
        p = [jnp.exp2(s[c] - m_new[c]) for c in comps]
        for c in comps:
            l_ref[c] = alpha[c] * l_ref[c] + jnp.sum(p[c], axis=0, keepdims=True)
            m_ref[c] = m_new[c]
        pv = [jnp.dot(v_t, p[c].astype(BF16), preferred_element_type=F32) for c in comps]
        for c in comps:
            acc_ref[c] = alpha[c] * acc_ref[c] + pv[c]

    def body(j, carry):
        step(j, None)
        return carry

    lax.fori_loop(0, i * diag_tiles, body, 0)
    for dj in range(diag_tiles):
        step(i * diag_tiles + dj, dj * tk)

    lam_rows = lam_ref[...]
    lam = (jnp.exp(jnp.sum(lam_rows[0:1] * lam_rows[1:2], axis=-1, keepdims=True))
           - jnp.exp(jnp.sum(lam_rows[2:3] * lam_rows[3:4], axis=-1, keepdims=True)) + lambda_init)
    o_t = acc_ref[0] / l_ref[0] - lam * (acc_ref[1] / l_ref[1])
    o = o_t.T
    o = o * lax.rsqrt(jnp.mean(o * o, axis=-1, keepdims=True) + LN_EPS) * g_ref[...]
    o_ref[...] = (o * (1.0 - lambda_init)).astype(o_ref.dtype)


def _diff_attention(proj, lam_rows, subln_g, *, lambda_init, tq, tk):
    b, s, n3 = proj.shape
    d_model = n3 // 3
    heads = d_model // DA_V_DIM
    kern = functools.partial(_da_kernel, tq=tq, tk=tk, lambda_init=lambda_init)
    return pl.pallas_call(
        kern,
        grid=(b, heads, s // tq),
        in_specs=[
            pl.BlockSpec((None, tq, DA_V_DIM), lambda bi, h, i: (bi, i, h)),
            pl.BlockSpec((None, s, DA_V_DIM), lambda bi, h, i: (bi, 0, heads + h)),
            pl.BlockSpec((None, s, DA_V_DIM), lambda bi, h, i: (bi, 0, 2 * heads + h)),
            pl.BlockSpec((4, HEAD_DIM), lambda bi, h, i: (0, 0)),
            pl.BlockSpec((1, DA_V_DIM), lambda bi, h, i: (0, 0)),
        ],
        out_specs=pl.BlockSpec((None, tq, DA_V_DIM), lambda bi, h, i: (bi, i, h)),
        out_shape=jax.ShapeDtypeStruct((b, s, d_model), BF16),
        scratch_shapes=[pltpu.VMEM((s // tk, DA_V_DIM, tk), BF16), pltpu.VMEM((2, DA_V_DIM, tq), F32),
                        pltpu.VMEM((2, 1, tq), F32), pltpu.VMEM((2, 1, tq), F32)],
        compiler_params=_cparams(("parallel", "parallel", "arbitrary")),
        name="diff_attention",
    )(proj, proj, proj, lam_rows, subln_g)


def _sb_kernel(q_ref, k_ref, v_ref, o_ref, vt_ref, acc_ref, run_ref, *, tq, tk, hp):
    i = pl.program_id(2)
    diag_tiles = tq // tk
    heads = range(hp)

    def head(x, h, axis):
        return x[:, h * HEAD_DIM:(h + 1) * HEAD_DIM] if axis == 1 else x[h * HEAD_DIM:(h + 1) * HEAD_DIM, :]

    @pl.when(i == 0)
    def _():
        _store_v_transposed(v_ref, vt_ref, tk)

    acc_ref[...] = jnp.zeros(acc_ref.shape, F32)
    run_ref[...] = jnp.zeros(run_ref.shape, F32)
    q_t = _transpose_bf16(q_ref[...])
    key = lax.broadcasted_iota(jnp.int32, (tk, tq), 0)
    qry = lax.broadcasted_iota(jnp.int32, (tk, tq), 1)
    k_a = lax.broadcasted_iota(jnp.int32, (tk, tk), 0)
    k_b = lax.broadcasted_iota(jnp.int32, (tk, tk), 1)
    later = (k_b > k_a).astype(BF16)

    def step(j, shift):
        off = pl.multiple_of(j * tk, tk)
        ks = k_ref[pl.ds(off, tk), :]
        v_t = vt_ref[j]
        z = [jnp.dot(head(ks, h, 1), head(q_t, h, 0), preferred_element_type=F32) for h in heads]
        log_beta = [jnp.minimum(zh, 0.0) - jnp.log2(1.0 + jnp.exp2(-jnp.abs(zh))) for zh in z]
        log_1m = [log_beta[h] - z[h] for h in heads]
        if shift is not None:
            strict = key + shift < qry
            log_1m = [jnp.where(strict, x, 0.0) for x in log_1m]
        after = [jnp.dot(later, x.astype(BF16), preferred_element_type=F32) for x in log_1m]
        run = [run_ref[h] for h in heads]
        a = [jnp.exp2(log_beta[h] + after[h] + run[h]) for h in heads]
        if shift is not None:
            a = [jnp.where(strict, x, 0.0) for x in a]
        for h in heads:
            acc_ref[h] += jnp.dot(head(v_t, h, 0), a[h].astype(BF16), preferred_element_type=F32)
            run_ref[h] = run[h] + after[h][0:1, :] + log_1m[h][0:1, :]

    for dj in reversed(range(diag_tiles)):
        step(i * diag_tiles + dj, dj * tk)

    def body(jj, carry):
        step(i * diag_tiles - 1 - jj, None)
        return carry

    lax.fori_loop(0, i * diag_tiles, body, 0)
    for h in heads:
        o_ref[:, h * HEAD_DIM:(h + 1) * HEAD_DIM] = acc_ref[h].T.astype(o_ref.dtype)


def _stick_breaking_attention(proj, *, tq, tk, hp):
    b, s, n3 = proj.shape
    d_model = n3 // 3
    heads = d_model // HEAD_DIM
    groups = heads // hp
    width = hp * HEAD_DIM
    kern = functools.partial(_sb_kernel, tq=tq, tk=tk, hp=hp)
    return pl.pallas_call(
        kern,
        grid=(b, groups, s // tq),
        in_specs=[
            pl.BlockSpec((None, tq, width), lambda bi, h, i: (bi, i, h)),
            pl.BlockSpec((None, s, width), lambda bi, h, i: (bi, 0, groups + h)),
            pl.BlockSpec((None, s, width), lambda bi, h, i: (bi, 0, 2 * groups + h)),
        ],
        out_specs=pl.BlockSpec((None, tq, width), lambda bi, h, i: (bi, i, h)),
        out_shape=jax.ShapeDtypeStruct((b, s, d_model), BF16),
        scratch_shapes=[pltpu.VMEM((s // tk, width, tk), BF16), pltpu.VMEM((hp, HEAD_DIM, tq), F32),
                        pltpu.VMEM((hp, 1, tq), F32)],
        compiler_params=_cparams(("parallel", "parallel", "arbitrary")),
        name="stick_breaking_attention",
    )(proj, proj, proj)


def _outproj_ln_kernel(o_ref, w_ref, h_ref, g_ref, b_ref, out_ref, *, alpha):
    mix = jnp.dot(o_ref[...], w_ref[...], preferred_element_type=F32)
    out_ref[...] = _layer_norm_rows(alpha * h_ref[...] + mix, g_ref[...], b_ref[...])


def _out_projection_ln(o2d, w_bf16, h2d, g, b, *, alpha, tm):
    m, d = h2d.shape
    kern = functools.partial(_outproj_ln_kernel, alpha=alpha)
    return pl.pallas_call(
        kern,
        grid=(m // tm,),
        in_specs=[
            pl.BlockSpec((tm, d), lambda i: (i, 0)),
            pl.BlockSpec((d, d), lambda i: (0, 0)),
            pl.BlockSpec((tm, d), lambda i: (i, 0)),
            pl.BlockSpec((1, d), lambda i: (0, 0)),
            pl.BlockSpec((1, d), lambda i: (0, 0)),
        ],
        out_specs=pl.BlockSpec((tm, d), lambda i: (i, 0)),
        out_shape=jax.ShapeDtypeStruct((m, d), F32),
        compiler_params=_cparams(("parallel",)),
        name="out_projection_ln",
    )(o2d, w_bf16, h2d, g, b)


def _router_kernel(h_ref, w_ref, b_ref, idx_ref, gate_ref, rank_ref, cnt_ref, carry_ref, *, n_experts):
    step = pl.program_id(0)

    @pl.when(step == 0)
    def _():
        carry_ref[...] = jnp.zeros(carry_ref.shape, F32)

    tm = h_ref.shape[0]
    logits = jnp.dot(h_ref[...], w_ref[...], preferred_element_type=F32,
                     precision=lax.Precision.HIGHEST) + b_ref[...]
    lane = lax.broadcasted_iota(jnp.int32, (tm, LANES), 1)
    work = jnp.where(lane < n_experts, logits, -jnp.inf)

    vals, idxs = [], []
    onehot = jnp.zeros((tm, LANES), F32)
    for _ in range(TOP_K):
        mx = jnp.max(work, axis=-1, keepdims=True)
        ix = jnp.min(jnp.where(work == mx, lane, LANES), axis=-1, keepdims=True)
        sel = lane == ix
        vals.append(mx)
        idxs.append(ix)
        onehot = jnp.where(sel, 1.0, onehot)
        work = jnp.where(sel, -jnp.inf, work)

    exps = [jnp.exp(v - vals[0]) for v in vals]
    denom = exps[0]
    for e in exps[1:]:
        denom = denom + e

    r_i = lax.broadcasted_iota(jnp.int32, (tm, tm), 0)
    c_i = lax.broadcasted_iota(jnp.int32, (tm, tm), 1)
    lower = (c_i < r_i).astype(BF16)
    earlier = jnp.dot(lower, onehot.astype(BF16), preferred_element_type=F32) + carry_ref[...]

    idx_out = jnp.zeros((tm, LANES), jnp.int32)
    gate_out = jnp.zeros((tm, LANES), F32)
    rank_out = jnp.zeros((tm, LANES), jnp.int32)
    for k in range(TOP_K):
        rk = jnp.sum(jnp.where(lane == idxs[k], earlier, 0.0), axis=-1, keepdims=True)
        idx_out = jnp.where(lane == k, idxs[k], idx_out)
        gate_out = jnp.where(lane == k, exps[k] / denom, gate_out)
        rank_out = jnp.where(lane == k, rk.astype(jnp.int32), rank_out)
    idx_ref[...] = idx_out
    gate_ref[...] = gate_out
    rank_ref[...] = rank_out

    total = carry_ref[...] + jnp.sum(onehot, axis=0, keepdims=True)
    carry_ref[...] = total
    cnt_ref[...] = total.astype(jnp.int32)


def _router(h2d, w_pad, b_pad, *, n_experts, tm):
    t, d = h2d.shape
    kern = functools.partial(_router_kernel, n_experts=n_experts)
    tok_spec = pl.BlockSpec((tm, LANES), lambda i: (i, 0))
    return pl.pallas_call(
        kern,
        grid=(t // tm,),
        in_specs=[
            pl.BlockSpec((tm, d), lambda i: (i, 0)),
            pl.BlockSpec((d, LANES), lambda i: (0, 0)),
            pl.BlockSpec((1, LANES), lambda i: (0, 0)),
        ],
        out_specs=[tok_spec, tok_spec, tok_spec, pl.BlockSpec((1, LANES), lambda i: (0, 0))],
        out_shape=[jax.ShapeDtypeStruct((t, LANES), jnp.int32), jax.ShapeDtypeStruct((t, LANES), F32),
                   jax.ShapeDtypeStruct((t, LANES), jnp.int32), jax.ShapeDtypeStruct((1, LANES), jnp.int32)],
        scratch_shapes=[pltpu.VMEM((1, LANES), F32)],
        compiler_params=_cparams(("arbitrary",)),
        name="router",
    )(h2d, w_pad, b_pad)


def _dispatch_kernel(dest_ref, h_ref, xin_hbm, x_hbm, sem, *, tm):
    del xin_hbm
    base = pl.program_id(0) * tm

    def row_copy(t, k):
        r = dest_ref[(base + t) * TOP_K + k]
        return pltpu.make_async_copy(h_ref.at[pl.ds(t, 1), :], x_hbm.at[pl.ds(r, 1), :], sem)

    def issue(t, carry):
        for k in range(TOP_K):
            row_copy(t, k).start()
        return carry

    lax.fori_loop(0, tm, issue, 0, unroll=DMA_ISSUE_UNROLL)
    for _ in range(TOP_K):
        pltpu.make_async_copy(h_ref, x_hbm.at[pl.ds(0, tm), :], sem).wait()


def _dispatch(dest_flat, h2d, x_zero, *, tm):
    t, d = h2d.shape
    kern = functools.partial(_dispatch_kernel, tm=tm)
    return pl.pallas_call(
        kern,
        grid_spec=pltpu.PrefetchScalarGridSpec(
            num_scalar_prefetch=1,
            grid=(t // tm,),
            in_specs=[pl.BlockSpec((tm, d), lambda i, dest: (i, 0)), pl.BlockSpec(memory_space=pl.ANY)],
            out_specs=pl.BlockSpec(memory_space=pl.ANY),
            scratch_shapes=[pltpu.SemaphoreType.DMA(())],
        ),
        out_shape=jax.ShapeDtypeStruct(x_zero.shape, x_zero.dtype),
        input_output_aliases={2: 0},
        compiler_params=_cparams(("arbitrary",)),
        name="dispatch",
    )(dest_flat, h2d, x_zero)


def _expert_kernel(ce_ref, cr_ref, cn_ref, x_hbm, w1g_ref, w1l_ref, b1g_ref, b1l_ref, w2_ref, b2_ref, y_hbm,
                   xb_ref, y_ref, stage_ref, w1g_b, w1l_b, w2_b, sem, *, nf):
    del ce_ref
    s = pl.program_id(0)
    item = jnp.maximum(s - 1, 0)
    c = item // nf
    f = item % nf
    nv = jnp.where(s > 0, cn_ref[c], 0)
    row0 = pl.multiple_of(cr_ref[c], SUBLANES)
    d = y_ref.shape[1]
    cur = (s + 1) % 2
    nxt = s % 2

    def cast_next_weights():
        w1g_b[nxt] = w1g_ref[...].astype(BF16)
        w1l_b[nxt] = w1l_ref[...].astype(BF16)
        w2_b[nxt] = w2_ref[...].astype(BF16)

    @pl.when(s == 0)
    def _():
        cast_next_weights()

    @pl.when((f == 0) & (nv > 0))
    def _():
        cp = pltpu.make_async_copy(x_hbm.at[pl.ds(row0, CHUNK_ROWS), :], stage_ref, sem.at[0])
        cp.start()
        cp.wait()
        xb_ref[...] = stage_ref[...].astype(BF16)
        y_ref[...] = jnp.broadcast_to(b2_ref[...], y_ref.shape)

    def compute(rows):
        cast_next_weights()
        blk = PIPE_ROWS if rows % PIPE_ROWS == 0 else rows
        col = min(512, d)

        def first(r0):
            xs = xb_ref[r0:r0 + blk, :]
            hg = jnp.dot(xs, w1g_b[cur], preferred_element_type=F32) + b1g_ref[...]
            hl = jnp.dot(xs, w1l_b[cur], preferred_element_type=F32) + b1l_ref[...]
            return hg, hl

        def activate(hg, hl):
            glu = jnp.minimum(hg, SWIGLU_LIMIT)
            lin = jnp.clip(hl, -SWIGLU_LIMIT, SWIGLU_LIMIT)
            return (glu * jax.nn.sigmoid(SWIGLU_ALPHA * glu) * (lin + 1.0)).astype(BF16)

        def second(r0, act):
            for n in range(d // col):
                y_ref[r0:r0 + blk, n * col:(n + 1) * col] += jnp.dot(
                    act, w2_b[cur, :, n * col:(n + 1) * col], preferred_element_type=F32)

        starts = list(range(0, rows, blk))
        h = first(starts[0])
        for k, r0 in enumerate(starts):
            act = activate(*h)
            if k + 1 < len(starts):
                h = first(starts[k + 1])
            second(r0, act)

    @pl.when(nv > SMALL_ROWS)
    def _():
        compute(CHUNK_ROWS)

    @pl.when((nv > 0) & (nv <= SMALL_ROWS))
    def _():
        compute(SMALL_ROWS)

    @pl.when((f == nf - 1) & (nv > 0))
    def _():
        sizes = []
        size = SUBLANES
        while size <= CHUNK_ROWS:
            sizes.append(size)
            size *= 2

        def piece(size):
            off = pl.multiple_of(nv - (nv % (2 * size)), SUBLANES)
            return pltpu.make_async_copy(y_ref.at[pl.ds(off, size), :], y_hbm.at[pl.ds(row0 + off, size), :],
                                         sem.at[1])

        for size in sizes:
            @pl.when((nv & size) != 0)
            def _():
                piece(size).start()
        for size in sizes:
            @pl.when((nv & size) != 0)
            def _():
                piece(size).wait()


def _expert_ffn(chunk_e, chunk_row, chunk_nv, x_buf, w1, b1, w2, b2, *, layer, tf):
    p, d = x_buf.shape
    n_experts, _, f2 = w1.shape[1:]
    d_ff = f2 // 2
    nf = d_ff // tf
    nc = chunk_e.shape[0]
    b1r = b1.reshape(b1.shape[0], n_experts, 1, f2)
    b2r = b2.reshape(b2.shape[0], n_experts, 1, d)

    n_items = nc * nf

    def expert_and_tile(item, ce, cn):
        c = item // nf
        return ce[c], jnp.where(cn[c] > 0, item % nf, nf - 1)

    def weights_of_next(s, ce, cn):
        return expert_and_tile(jnp.minimum(s, n_items - 1), ce, cn)

    def biases_of_current(s, ce, cn):
        return expert_and_tile(jnp.maximum(s - 1, 0), ce, cn)

    def w1_map(half):
        def index(s, ce, cr, cn):
            e, f = weights_of_next(s, ce, cn)
            return layer, e, 0, half * nf + f
        return index

    def b1_map(half):
        def index(s, ce, cr, cn):
            e, f = biases_of_current(s, ce, cn)
            return layer, e, 0, half * nf + f
        return index

    def w2_map(s, ce, cr, cn):
        e, f = weights_of_next(s, ce, cn)
        return layer, e, f, 0

    def b2_map(s, ce, cr, cn):
        e, _ = biases_of_current(s, ce, cn)
        return layer, e, 0, 0

    kern = functools.partial(_expert_kernel, nf=nf)
    return pl.pallas_call(
        kern,
        grid_spec=pltpu.PrefetchScalarGridSpec(
            num_scalar_prefetch=3,
            grid=(n_items + 1,),
            in_specs=[
                pl.BlockSpec(memory_space=pl.ANY),
                pl.BlockSpec((None, None, d, tf), w1_map(0)),
                pl.BlockSpec((None, None, d, tf), w1_map(1)),
                pl.BlockSpec((None, None, 1, tf), b1_map(0)),
                pl.BlockSpec((None, None, 1, tf), b1_map(1)),
                pl.BlockSpec((None, None, tf, d), w2_map),
                pl.BlockSpec((None, None, 1, d), b2_map),
            ],
            out_specs=pl.BlockSpec(memory_space=pl.ANY),
            scratch_shapes=[
                pltpu.VMEM((CHUNK_ROWS, d), BF16),
                pltpu.VMEM((CHUNK_ROWS, d), F32),
                pltpu.VMEM((CHUNK_ROWS, d), F32),
                pltpu.VMEM((2, d, tf), BF16),
                pltpu.VMEM((2, d, tf), BF16),
                pltpu.VMEM((2, tf, d), BF16),
                pltpu.SemaphoreType.DMA((2,)),
            ],
        ),
        out_shape=jax.ShapeDtypeStruct((p, d), F32),
        input_output_aliases={3: 0},
        compiler_params=_cparams(("arbitrary",)),
        name="expert_ffn",
    )(chunk_e, chunk_row, chunk_nv, x_buf, w1, w1, b1r, b1r, w2, b2r)


def _combine_kernel(dest_ref, y_hbm, gate_ref, h_ref, g_ref, b_ref, out_ref, buf_ref, sem, *, tm, alpha):
    base = pl.program_id(0) * tm

    def row_copy(t, k):
        r = dest_ref[(base + t) * TOP_K + k]
        return pltpu.make_async_copy(y_hbm.at[pl.ds(r, 1), :], buf_ref.at[k, pl.ds(t, 1), :], sem)

    def issue(t, carry):
        for k in range(TOP_K):
            row_copy(t, k).start()
        return carry

    lax.fori_loop(0, tm, issue, 0, unroll=DMA_ISSUE_UNROLL)
    for k in range(TOP_K):
        pltpu.make_async_copy(y_hbm.at[pl.ds(0, tm), :], buf_ref.at[k], sem).wait()

    gates = gate_ref[...]
    y = gates[:, 0:1] * buf_ref[0]
    for k in range(1, TOP_K):
        y = y + gates[:, k:k + 1] * buf_ref[k]
    out_ref[...] = _layer_norm_rows(alpha * h_ref[...] + y, g_ref[...], b_ref[...])


def _combine_ln(dest_flat, y_buf, gates, h2d, g, b, *, alpha, tm):
    t, d = h2d.shape
    kern = functools.partial(_combine_kernel, tm=tm, alpha=alpha)
    return pl.pallas_call(
        kern,
        grid_spec=pltpu.PrefetchScalarGridSpec(
            num_scalar_prefetch=1,
            grid=(t // tm,),
            in_specs=[
                pl.BlockSpec(memory_space=pl.ANY),
                pl.BlockSpec((tm, LANES), lambda i, dest: (i, 0)),
                pl.BlockSpec((tm, d), lambda i, dest: (i, 0)),
                pl.BlockSpec((1, d), lambda i, dest: (0, 0)),
                pl.BlockSpec((1, d), lambda i, dest: (0, 0)),
            ],
            out_specs=pl.BlockSpec((tm, d), lambda i, dest: (i, 0)),
            scratch_shapes=[pltpu.VMEM((TOP_K, tm, d), F32), pltpu.SemaphoreType.DMA(())],
        ),
        out_shape=jax.ShapeDtypeStruct((t, d), F32),
        compiler_params=_cparams(("arbitrary",)),
        name="combine_ln",
    )(dest_flat, y_buf, gates, h2d, g, b)


def _routing_tables(idx, rank, counts, *, n_chunks):
    padded = (counts + SUBLANES - 1) // SUBLANES * SUBLANES
    pad_end = jnp.cumsum(padded)
    pad_start = pad_end - padded
    dest = pad_start[idx] + rank
    nchunk = (padded + CHUNK_ROWS - 1) // CHUNK_ROWS
    chunk_end = jnp.cumsum(nchunk)
    chunk_start = chunk_end - nchunk
    total = chunk_end[-1]
    cids = jnp.arange(n_chunks, dtype=jnp.int32)
    clamped = jnp.minimum(cids, total - 1)
    ce = jnp.sum((chunk_end[None, :] <= clamped[:, None]).astype(jnp.int32), axis=1)
    local = clamped - chunk_start[ce]
    crow = (pad_start[ce] + local * CHUNK_ROWS).astype(jnp.int32)
    cnv = jnp.clip(padded[ce] - local * CHUNK_ROWS, 0, CHUNK_ROWS)
    cnv = jnp.where(cids < total, cnv, 0).astype(jnp.int32)
    return dest.reshape(-1).astype(jnp.int32), ce.astype(jnp.int32), crow, cnv


def _moe_ln(h2d, router_w, router_b, w1, b1, w2, b2, g, b, *, layer, alpha, tiles):
    t, d = h2d.shape
    n_experts = router_w.shape[1]
    w_pad = jnp.pad(router_w, ((0, 0), (0, LANES - n_experts)))
    b_pad = jnp.pad(router_b, (0, LANES - n_experts)).reshape(1, LANES)
    idx, gates, rank, counts = _router(h2d, w_pad, b_pad, n_experts=n_experts, tm=tiles["router_tm"])
    n_chunks = (t * TOP_K + n_experts * SUBLANES) // CHUNK_ROWS + n_experts
    dest_flat, ce, crow, cnv = _routing_tables(idx[:, :TOP_K], rank[:, :TOP_K], counts[0, :n_experts],
                                               n_chunks=n_chunks)
    p_rows = t * TOP_K + n_experts * SUBLANES + CHUNK_ROWS
    x_buf = _dispatch(dest_flat, h2d, jnp.zeros((p_rows, d), F32), tm=tiles["dispatch_tm"])
    y_buf = _expert_ffn(ce, crow, cnv, x_buf, w1, b1, w2, b2, layer=layer, tf=tiles["expert_tf"])
    return _combine_ln(dest_flat, y_buf, gates, h2d, g, b, alpha=alpha, tm=tiles["combine_tm"])


def _rope_tables(positions, q_scale):
    inv_freq = ROPE_THETA ** (-jnp.arange(0, HEAD_DIM, 2, dtype=F32) / HEAD_DIM)
    ang = positions.astype(F32).reshape(-1)[:, None] * inv_freq
    cos, sin = jnp.cos(ang), jnp.sin(ang)
    cos2 = jnp.concatenate([cos, cos], axis=-1)
    sin2 = jnp.concatenate([-sin, sin], axis=-1)
    k_tab = jnp.stack([cos2, sin2])
    return jnp.stack([k_tab * q_scale, k_tab])


def _tiles(t, s, d, d_ff):
    return {
        "inproj_tm": min(1024, t), "inproj_tn": min(1024, d),
        "da_tq": min(512, s), "da_tk": min(512, s),
        "sb_tq": min(512, s), "sb_tk": min(256, s), "sb_heads_per_step": min(4, d // HEAD_DIM),
        "outproj_tm": min(256, t),
        "router_tm": min(256, t),
        "dispatch_tm": min(256, t),
        "combine_tm": min(128, t),
        "expert_tf": min(256, d_ff),
    }


def kernel(x, positions, da_w_in, da_lambda_q1, da_lambda_k1, da_lambda_q2, da_lambda_k2, da_subln_g, da_w_out,
           sb_w_in, sb_w_out, ln1_g, ln1_b, ln2_g, ln2_b, router_w, router_b, expert_w1, expert_b1, expert_w2,
           expert_b2):
    bsz, seq, d = x.shape
    t = bsz * seq
    depth = ln1_g.shape[0]
    d_ff = expert_w2.shape[2]
    alpha = (2 * depth) ** 0.25
    q_scale = HEAD_DIM ** -0.5 * math.log2(math.e)
    tiles = _tiles(t, seq, d, d_ff)
    tab = _rope_tables(positions, q_scale)

    h = x.reshape(t, d)
    for i in range(depth):
        j = i // 2
        if i % 2 == 0:
            lambda_init = 0.8 - 0.6 * math.exp(-0.3 * i)
            proj = _in_projection(h, da_w_in[j].astype(BF16), tab, rope=True, d_model=d, q_scale=q_scale,
                                  tm=tiles["inproj_tm"], tn=tiles["inproj_tn"])
            lam_rows = jnp.stack([da_lambda_q1[j], da_lambda_k1[j], da_lambda_q2[j], da_lambda_k2[j]])
            o = _diff_attention(proj.reshape(bsz, seq, 3 * d), lam_rows, da_subln_g[j].reshape(1, DA_V_DIM),
                                lambda_init=lambda_init, tq=tiles["da_tq"], tk=tiles["da_tk"])
            w_out = da_w_out[j]
        else:
            proj = _in_projection(h, sb_w_in[j].astype(BF16), tab, rope=False, d_model=d, q_scale=q_scale,
                                  tm=tiles["inproj_tm"], tn=tiles["inproj_tn"])
            o = _stick_breaking_attention(proj.reshape(bsz, seq, 3 * d), tq=tiles["sb_tq"], tk=tiles["sb_tk"],
                                          hp=tiles["sb_heads_per_step"])
            w_out = sb_w_out[j]
        h = _out_projection_ln(o.reshape(t, d), w_out.astype(BF16), h, ln1_g[i].reshape(1, d),
                               ln1_b[i].reshape(1, d), alpha=alpha, tm=tiles["outproj_tm"])
        h = _moe_ln(h, router_w[i], router_b[i], expert_w1, expert_b1, expert_w2, expert_b2,
                    ln2_g[i].reshape(1, d), ln2_b[i].reshape(1, d), layer=i, alpha=alpha, tiles=tiles)
    return h.reshape(bsz, seq, d)
```

```python
import functools
import math

import jax
import jax.numpy as jnp
from jax import lax
from jax.experimental import pallas as pl
from jax.experimental.pallas import tpu as pltpu

F32 = jnp.float32
BF16 = jnp.bfloat16

HEAD_DIM = 128
DA_V_DIM = 2 * HEAD_DIM
TOP_K = 4
SWIGLU_ALPHA = 1.702
SWIGLU_LIMIT = 7.0
ROPE_THETA = 10000.0
LN_EPS = 1e-5

LANES = 128
SUBLANES = 8
VMEM_LIMIT_BYTES = 56 * 1024 * 1024

CHUNK_ROWS = 1088
SMALL_ROWS = 128
PIPE_ROWS = CHUNK_ROWS
ZERO_ROWS = 64
DMA_ISSUE_UNROLL = 8


def _cparams(sem):
    return pltpu.CompilerParams(dimension_semantics=sem, vmem_limit_bytes=VMEM_LIMIT_BYTES)


def _layer_norm_rows(z, g, b):
    mu = jnp.mean(z, axis=-1, keepdims=True)
    zc = z - mu
    var = jnp.mean(zc * zc, axis=-1, keepdims=True)
    return zc * lax.rsqrt(var + LN_EPS) * g + b


def _inproj_kernel(x_ref, w_ref, tab_ref, o_ref, xb_ref, *, rope, qk_tiles, q_tiles, q_scale):
    j = pl.program_id(1)

    @pl.when(j == 0)
    def _():
        xb_ref[...] = x_ref[...].astype(BF16)

    acc = jnp.dot(xb_ref[...], w_ref[...], preferred_element_type=F32)
    tn = acc.shape[1]

    if rope:
        @pl.when(j < qk_tiles)
        def _():
            cos = tab_ref[0]
            sin = tab_ref[1]
            for c in range(tn // LANES):
                xc = acc[:, c * LANES:(c + 1) * LANES]
                rot = pltpu.roll(xc, LANES // 2, 1)
                o_ref[:, c * LANES:(c + 1) * LANES] = (xc * cos + rot * sin).astype(o_ref.dtype)

        @pl.when(j >= qk_tiles)
        def _():
            o_ref[...] = acc.astype(o_ref.dtype)
    else:
        @pl.when(j < q_tiles)
        def _():
            o_ref[...] = (acc * q_scale).astype(o_ref.dtype)

        @pl.when(j >= q_tiles)
        def _():
            o_ref[...] = acc.astype(o_ref.dtype)


def _in_projection(x2d, w_bf16, tab, *, rope, d_model, q_scale, tm, tn):
    m, k = x2d.shape
    n = w_bf16.shape[1]
    q_tiles = d_model // tn
    qk_tiles = 2 * q_tiles
    kern = functools.partial(_inproj_kernel, rope=rope, qk_tiles=qk_tiles, q_tiles=q_tiles, q_scale=q_scale)
    return pl.pallas_call(
        kern,
        grid=(m // tm, n // tn),
        in_specs=[
            pl.BlockSpec((tm, k), lambda i, j: (i, 0)),
            pl.BlockSpec((k, tn), lambda i, j: (0, j)),
            pl.BlockSpec((None, 2, tm, LANES), lambda i, j: (jnp.minimum(j // q_tiles, 1), 0, i, 0)),
        ],
        out_specs=pl.BlockSpec((tm, tn), lambda i, j: (i, j)),
        out_shape=jax.ShapeDtypeStruct((m, n), BF16),
        scratch_shapes=[pltpu.VMEM((tm, k), BF16)],
        compiler_params=_cparams(("parallel", "arbitrary")),
        name="in_projection",
    )(x2d, w_bf16, tab)


def _transpose_bf16(x):
    return x.astype(F32).T.astype(BF16)


def _store_v_transposed(v_ref, vt_ref, tk):
    for n in range(vt_ref.shape[0]):
        vt_ref[n] = _transpose_bf16(v_ref[n * tk:(n + 1) * tk, :])


def _da_kernel(q_ref, k_ref, v_ref, lam_ref, g_ref, o_ref, vt_ref, acc_ref, m_ref, l_ref, *, tq, tk, lambda_init):
    i = pl.program_id(2)
    diag_tiles = tq // tk

    @pl.when(i == 0)
    def _():
        _store_v_transposed(v_ref, vt_ref, tk)

    m_ref[...] = jnp.full(m_ref.shape, -jnp.inf, F32)
    l_ref[...] = jnp.zeros(l_ref.shape, F32)
    acc_ref[...] = jnp.zeros(acc_ref.shape, F32)
    q_t = _transpose_bf16(q_ref[...])
    key = lax.broadcasted_iota(jnp.int32, (tk, tq), 0)
    qry = lax.broadcasted_iota(jnp.int32, (tk, tq), 1)

    def step(j, shift):
        off = pl.multiple_of(j * tk, tk)
        ks = k_ref[pl.ds(off, tk), :]
        v_t = vt_ref[j]
        comps = range(2)
        s = [jnp.dot(ks[:, c * HEAD_DIM:(c + 1) * HEAD_DIM], q_t[c * HEAD_DIM:(c + 1) * HEAD_DIM, :],
                     preferred_element_type=F32) for c in comps]
        if shift is not None:
            visible = key + shift <= qry
            s = [jnp.where(visible, sc, -jnp.inf) for sc in s]
        m_old = [m_ref[c] for c in comps]
        m_new = [jnp.maximum(m_old[c], jnp.max(s[c], axis=0, keepdims=True)) for c in comps]
        alpha = [jnp.exp2(m_old[c] - m_new[c]) for c in comps]
        p = [jnp.exp2(s[c] - m_new[c]) for c in comps]
        for c in comps:
            l_ref[c] = alpha[c] * l_ref[c] + jnp.sum(p[c], axis=0, keepdims=True)
            m_ref[c] = m_new[c]
        pv = [jnp.dot(v_t, p[c].astype(BF16), preferred_element_type=F32) for c in comps]
        for c in comps:
            acc_ref[c] = alpha[c] * acc_ref[c] + pv[c]

    def body(j, carry):
        step(j, None)
        return carry

    lax.fori_loop(0, i * diag_tiles, body, 0)
    for dj in range(diag_tiles):
        step(i * diag_tiles + dj, dj * tk)

    lam_rows = lam_ref[...]
    lam = (jnp.exp(jnp.sum(lam_rows[0:1] * lam_rows[1:2], axis=-1, keepdims=True))
           - jnp.exp(jnp.sum(lam_rows[2:3] * lam_rows[3:4], axis=-1, keepdims=True)) + lambda_init)
    o_t = acc_ref[0] / l_ref[0] - lam * (acc_ref[1] / l_ref[1])
    o = o_t.T
    o = o * lax.rsqrt(jnp.mean(o * o, axis=-1, keepdims=True) + LN_EPS) * g_ref[...]
    o_ref[...] = (o * (1.0 - lambda_init)).astype(o_ref.dtype)


def _diff_attention(proj, lam_rows, subln_g, *, lambda_init, tq, tk):
    b, s, n3 = proj.shape
    d_model = n3 // 3
    heads = d_model // DA_V_DIM
    kern = functools.partial(_da_kernel, tq=tq, tk=tk, lambda_init=lambda_init)
    return pl.pallas_call(
        kern,
        grid=(b, heads, s // tq),
        in_specs=[
            pl.BlockSpec((None, tq, DA_V_DIM), lambda bi, h, i: (bi, i, h)),
            pl.BlockSpec((None, s, DA_V_DIM), lambda bi, h, i: (bi, 0, heads + h)),
            pl.BlockSpec((None, s, DA_V_DIM), lambda bi, h, i: (bi, 0, 2 * heads + h)),
            pl.BlockSpec((4, HEAD_DIM), lambda bi, h, i: (0, 0)),
            pl.BlockSpec((1, DA_V_DIM), lambda bi, h, i: (0, 0)),
        ],
        out_specs=pl.BlockSpec((None, tq, DA_V_DIM), lambda bi, h, i: (bi, i, h)),
        out_shape=jax.ShapeDtypeStruct((b, s, d_model), BF16),
        scratch_shapes=[pltpu.VMEM((s // tk, DA_V_DIM, tk), BF16), pltpu.VMEM((2, DA_V_DIM, tq), F32),
                        pltpu.VMEM((2, 1, tq), F32), pltpu.VMEM((2, 1, tq), F32)],
        compiler_params=_cparams(("parallel", "parallel", "arbitrary")),
        name="diff_attention",
    )(proj, proj, proj, lam_rows, subln_g)


def _sb_kernel(q_ref, k_ref, v_ref, o_ref, vt_ref, acc_ref, run_ref, *, tq, tk, hp):
    i = pl.program_id(2)
    diag_tiles = tq // tk
    heads = range(hp)

    def head(x, h, axis):
        return x[:, h * HEAD_DIM:(h + 1) * HEAD_DIM] if axis == 1 else x[h * HEAD_DIM:(h + 1) * HEAD_DIM, :]

    @pl.when(i == 0)
    def _():
        _store_v_transposed(v_ref, vt_ref, tk)

    acc_ref[...] = jnp.zeros(acc_ref.shape, F32)
    run_ref[...] = jnp.zeros(run_ref.shape, F32)
    q_t = _transpose_bf16(q_ref[...])
    key = lax.broadcasted_iota(jnp.int32, (tk, tq), 0)
    qry = lax.broadcasted_iota(jnp.int32, (tk, tq), 1)
    k_a = lax.broadcasted_iota(jnp.int32, (tk, tk), 0)
    k_b = lax.broadcasted_iota(jnp.int32, (tk, tk), 1)
    later = (k_b > k_a).astype(BF16)

    def step(j, shift):
        off = pl.multiple_of(j * tk, tk)
        ks = k_ref[pl.ds(off, tk), :]
        v_t = vt_ref[j]
        z = [jnp.dot(head(ks, h, 1), head(q_t, h, 0), preferred_element_type=F32) for h in heads]
        log_beta = [jnp.minimum(zh, 0.0) - jnp.log2(1.0 + jnp.exp2(-jnp.abs(zh))) for zh in z]
        log_1m = [log_beta[h] - z[h] for h in heads]
        if shift is not None:
            strict = key + shift < qry
            log_1m = [jnp.where(strict, x, 0.0) for x in log_1m]
        after = [jnp.dot(later, x.astype(BF16), preferred_element_type=F32) for x in log_1m]
        run = [run_ref[h] for h in heads]
        a = [jnp.exp2(log_beta[h] + after[h] + run[h]) for h in heads]
        if shift is not None:
            a = [jnp.where(strict, x, 0.0) for x in a]
        for h in heads:
            acc_ref[h] += jnp.dot(head(v_t, h, 0), a[h].astype(BF16), preferred_element_type=F32)
            run_ref[h] = run[h] + after[h][0:1, :] + log_1m[h][0:1, :]

    for dj in reversed(range(diag_tiles)):
        step(i * diag_tiles + dj, dj * tk)

    def body(jj, carry):
        step(i * diag_tiles - 1 - jj, None)
        return carry

    lax.fori_loop(0, i * diag_tiles, body, 0)
    for h in heads:
        o_ref[:, h * HEAD_DIM:(h + 1) * HEAD_DIM] = acc_ref[h].T.astype(o_ref.dtype)


def _stick_breaking_attention(proj, *, tq, tk, hp):
    b, s, n3 = proj.shape
    d_model = n3 // 3
    heads = d_model // HEAD_DIM
    groups = heads // hp
    width = hp * HEAD_DIM
    kern = functools.partial(_sb_kernel, tq=tq, tk=tk, hp=hp)
    return pl.pallas_call(
        kern,
        grid=(b, groups, s // tq),
        in_specs=[
            pl.BlockSpec((None, tq, width), lambda bi, h, i: (bi, i, h)),
            pl.BlockSpec((None, s, width), lambda bi, h, i: (bi, 0, groups + h)),
            pl.BlockSpec((None, s, width), lambda bi, h, i: (bi, 0, 2 * groups + h)),
        ],
        out_specs=pl.BlockSpec((None, tq, width), lambda bi, h, i: (bi, i, h)),
        out_shape=jax.ShapeDtypeStruct((b, s, d_model), BF16),
        scratch_shapes=[pltpu.VMEM((s // tk, width, tk), BF16), pltpu.VMEM((hp, HEAD_DIM, tq), F32),
                        pltpu.VMEM((hp, 1, tq), F32)],
        compiler_params=_cparams(("parallel", "parallel", "arbitrary")),
        name="stick_breaking_attention",
    )(proj, proj, proj)


def _outproj_ln_kernel(o_ref, w_ref, h_ref, g_ref, b_ref, out_ref, *, alpha):
    mix = jnp.dot(o_ref[...], w_ref[...], preferred_element_type=F32)
    out_ref[...] = _layer_norm_rows(alpha * h_ref[...] + mix, g_ref[...], b_ref[...])


def _out_projection_ln(o2d, w_bf16, h2d, g, b, *, alpha, tm):
    m, d = h2d.shape
    kern = functools.partial(_outproj_ln_kernel, alpha=alpha)
    return pl.pallas_call(
        kern,
        grid=(m // tm,),
        in_specs=[
            pl.BlockSpec((tm, d), lambda i: (i, 0)),
            pl.BlockSpec((d, d), lambda i: (0, 0)),
            pl.BlockSpec((tm, d), lambda i: (i, 0)),
            pl.BlockSpec((1, d), lambda i: (0, 0)),
            pl.BlockSpec((1, d), lambda i: (0, 0)),
        ],
        out_specs=pl.BlockSpec((tm, d), lambda i: (i, 0)),
        out_shape=jax.ShapeDtypeStruct((m, d), F32),
        compiler_params=_cparams(("parallel",)),
        name="out_projection_ln",
    )(o2d, w_bf16, h2d, g, b)


def _router_kernel(h_ref, w_ref, b_ref, idx_ref, gate_ref, rank_ref, cnt_ref, carry_ref, *, n_experts):
    step = pl.program_id(0)

    @pl.when(step == 0)
    def _():
        carry_ref[...] = jnp.zeros(carry_ref.shape, F32)

    tm = h_ref.shape[0]
    logits = jnp.dot(h_ref[...], w_ref[...], preferred_element_type=F32,
                     precision=lax.Precision.HIGHEST) + b_ref[...]
    lane = lax.broadcasted_iota(jnp.int32, (tm, LANES), 1)
    work = jnp.where(lane < n_experts, logits, -jnp.inf)

    vals, idxs = [], []
    onehot = jnp.zeros((tm, LANES), F32)
    for _ in range(TOP_K):
        mx = jnp.max(work, axis=-1, keepdims=True)
        ix = jnp.min(jnp.where(work == mx, lane, LANES), axis=-1, keepdims=True)
        sel = lane == ix
        vals.append(mx)
        idxs.append(ix)
        onehot = jnp.where(sel, 1.0, onehot)
        work = jnp.where(sel, -jnp.inf, work)

    exps = [jnp.exp(v - vals[0]) for v in vals]
    denom = exps[0]
    for e in exps[1:]:
        denom = denom + e

    r_i = lax.broadcasted_iota(jnp.int32, (tm, tm), 0)
    c_i = lax.broadcasted_iota(jnp.int32, (tm, tm), 1)
    lower = (c_i < r_i).astype(BF16)
    earlier = jnp.dot(lower, onehot.astype(BF16), preferred_element_type=F32) + carry_ref[...]

    idx_out = jnp.zeros((tm, LANES), jnp.int32)
    gate_out = jnp.zeros((tm, LANES), F32)
    rank_out = jnp.zeros((tm, LANES), jnp.int32)
    for k in range(TOP_K):
        rk = jnp.sum(jnp.where(lane == idxs[k], earlier, 0.0), axis=-1, keepdims=True)
        idx_out = jnp.where(lane == k, idxs[k], idx_out)
        gate_out = jnp.where(lane == k, exps[k] / denom, gate_out)
        rank_out = jnp.where(lane == k, rk.astype(jnp.int32), rank_out)
    idx_ref[...] = idx_out
    gate_ref[...] = gate_out
    rank_ref[...] = rank_out

    total = carry_ref[...] + jnp.sum(onehot, axis=0, keepdims=True)
    carry_ref[...] = total
    cnt_ref[...] = total.astype(jnp.int32)


def _router(h2d, w_pad, b_pad, *, n_experts, tm):
    t, d = h2d.shape
    kern = functools.partial(_router_kernel, n_experts=n_experts)
    tok_spec = pl.BlockSpec((tm, LANES), lambda i: (i, 0))
    return pl.pallas_call(
        kern,
        grid=(t // tm,),
        in_specs=[
            pl.BlockSpec((tm, d), lambda i: (i, 0)),
            pl.BlockSpec((d, LANES), lambda i: (0, 0)),
            pl.BlockSpec((1, LANES), lambda i: (0, 0)),
        ],
        out_specs=[tok_spec, tok_spec, tok_spec, pl.BlockSpec((1, LANES), lambda i: (0, 0))],
        out_shape=[jax.ShapeDtypeStruct((t, LANES), jnp.int32), jax.ShapeDtypeStruct((t, LANES), F32),
                   jax.ShapeDtypeStruct((t, LANES), jnp.int32), jax.ShapeDtypeStruct((1, LANES), jnp.int32)],
        scratch_shapes=[pltpu.VMEM((1, LANES), F32)],
        compiler_params=_cparams(("arbitrary",)),
        name="router",
    )(h2d, w_pad, b_pad)


def _dispatch_kernel(dest_ref, pad_ref, h_ref, x_hbm, zero_ref, sem, *, tm, tail_start, tail_rows):
    step = pl.program_id(0)
    base = step * tm

    @pl.when(step == 0)
    def _():
        zero_ref[...] = jnp.zeros(zero_ref.shape, F32)

        def tail_copy(n):
            return pltpu.make_async_copy(zero_ref, x_hbm.at[pl.ds(tail_start + n * ZERO_ROWS, ZERO_ROWS), :], sem)

        for n in range(tail_rows // ZERO_ROWS):
            tail_copy(n).start()
        for n in range(tail_rows // ZERO_ROWS):
            tail_copy(n).wait()

        def pad_copy(i):
            return pltpu.make_async_copy(h_ref.at[pl.ds(0, 1), :], x_hbm.at[pl.ds(pad_ref[i], 1), :], sem)

        def pad_start(i, carry):
            @pl.when(pad_ref[i] >= 0)
            def _():
                pad_copy(i).start()
            return carry

        def pad_wait(i, carry):
            @pl.when(pad_ref[i] >= 0)
            def _():
                pad_copy(i).wait()
            return carry

        lax.fori_loop(0, pad_ref.shape[0], pad_start, 0)
        lax.fori_loop(0, pad_ref.shape[0], pad_wait, 0)

    def row_copy(t, k):
        r = dest_ref[(base + t) * TOP_K + k]
        return pltpu.make_async_copy(h_ref.at[pl.ds(t, 1), :], x_hbm.at[pl.ds(r, 1), :], sem)

    def issue(t, carry):
        for k in range(TOP_K):
            row_copy(t, k).start()
        return carry

    lax.fori_loop(0, tm, issue, 0, unroll=DMA_ISSUE_UNROLL)
    for _ in range(TOP_K):
        pltpu.make_async_copy(h_ref, x_hbm.at[pl.ds(0, tm), :], sem).wait()


def _dispatch(dest_flat, pad_rows, h2d, *, p_rows, tm):
    t, d = h2d.shape
    tail_start = t * TOP_K
    tail_rows = p_rows - tail_start
    assert tail_rows % ZERO_ROWS == 0
    kern = functools.partial(_dispatch_kernel, tm=tm, tail_start=tail_start, tail_rows=tail_rows)
    return pl.pallas_call(
        kern,
        grid_spec=pltpu.PrefetchScalarGridSpec(
            num_scalar_prefetch=2,
            grid=(t // tm,),
            in_specs=[pl.BlockSpec((tm, d), lambda i, dest, pad: (i, 0))],
            out_specs=pl.BlockSpec(memory_space=pl.ANY),
            scratch_shapes=[pltpu.VMEM((ZERO_ROWS, d), F32), pltpu.SemaphoreType.DMA(())],
        ),
        out_shape=jax.ShapeDtypeStruct((p_rows, d), F32),
        compiler_params=_cparams(("arbitrary",)),
        name="dispatch",
    )(dest_flat, pad_rows, h2d)


def _expert_kernel(ce_ref, cr_ref, cn_ref, x_hbm, w1g_ref, w1l_ref, b1g_ref, b1l_ref, w2_ref, b2_ref, y_hbm,
                   xb_ref, y_ref, stage_ref, w1g_b, w1l_b, w2_b, sem, *, nf, n_chunks):
    del ce_ref
    s = pl.program_id(0)
    item = jnp.maximum(s - 1, 0)
    c = item // nf
    f = item % nf
    nv = jnp.where(s > 0, cn_ref[c], 0)
    row0 = pl.multiple_of(cr_ref[c], SUBLANES)
    d = y_ref.shape[1]
    cur = (s + 1) % 2
    nxt = s % 2

    def cast_next_weights():
        w1g_b[nxt] = w1g_ref[...].astype(BF16)
        w1l_b[nxt] = w1l_ref[...].astype(BF16)
        w2_b[nxt] = w2_ref[...].astype(BF16)

    @pl.when(s == 0)
    def _():
        cast_next_weights()

    def x_copy(chunk):
        start = pl.multiple_of(cr_ref[chunk], SUBLANES)
        return pltpu.make_async_copy(x_hbm.at[pl.ds(start, CHUNK_ROWS), :], stage_ref, sem.at[0])

    @pl.when((f == 0) & (nv > 0) & (c == 0))
    def _():
        x_copy(c).start()

    @pl.when((f == 0) & (nv > 0))
    def _():
        x_copy(c).wait()
        xb_ref[...] = stage_ref[...].astype(BF16)
        y_ref[...] = jnp.broadcast_to(b2_ref[...], y_ref.shape)

    c_next = jnp.minimum(c + 1, n_chunks - 1)

    @pl.when((f == 1) & (nv > 0) & (c + 1 < n_chunks) & (cn_ref[c_next] > 0))
    def _():
        x_copy(c_next).start()

    def compute(rows):
        cast_next_weights()
        blk = PIPE_ROWS if rows % PIPE_ROWS == 0 else rows
        col = min(512, d)

        def first(r0):
            xs = xb_ref[r0:r0 + blk, :]
            hg = jnp.dot(xs, w1g_b[cur], preferred_element_type=F32) + b1g_ref[...]
            hl = jnp.dot(xs, w1l_b[cur], preferred_element_type=F32) + b1l_ref[...]
            return hg, hl

        def activate(hg, hl):
            glu = jnp.minimum(hg, SWIGLU_LIMIT)
            lin = jnp.clip(hl, -SWIGLU_LIMIT, SWIGLU_LIMIT)
            return (glu * jax.nn.sigmoid(SWIGLU_ALPHA * glu) * (lin + 1.0)).astype(BF16)

        def second(r0, act):
            for n in range(d // col):
                y_ref[r0:r0 + blk, n * col:(n + 1) * col] += jnp.dot(
                    act, w2_b[cur, :, n * col:(n + 1) * col], preferred_element_type=F32)

        starts = list(range(0, rows, blk))
        h = first(starts[0])
        for k, r0 in enumerate(starts):
            act = activate(*h)
            if k + 1 < len(starts):
                h = first(starts[k + 1])
            second(r0, act)

    @pl.when(nv > SMALL_ROWS)
    def _():
        compute(CHUNK_ROWS)

    @pl.when((nv > 0) & (nv <= SMALL_ROWS))
    def _():
        compute(SMALL_ROWS)

    @pl.when((f == nf - 1) & (nv > 0))
    def _():
        sizes = []
        size = SUBLANES
        while size <= CHUNK_ROWS:
            sizes.append(size)
            size *= 2

        def piece(size):
            off = pl.multiple_of(nv - (nv % (2 * size)), SUBLANES)
            return pltpu.make_async_copy(y_ref.at[pl.ds(off, size), :], y_hbm.at[pl.ds(row0 + off, size), :],
                                         sem.at[1])

        for size in sizes:
            @pl.when((nv & size) != 0)
            def _():
                piece(size).start()
        for size in sizes:
            @pl.when((nv & size) != 0)
            def _():
                piece(size).wait()


def _expert_ffn(chunk_e, chunk_row, chunk_nv, x_buf, w1, b1, w2, b2, *, layer, tf):
    p, d = x_buf.shape
    n_experts, _, f2 = w1.shape[1:]
    d_ff = f2 // 2
    nf = d_ff // tf
    nc = chunk_e.shape[0]
    b1r = b1.reshape(b1.shape[0], n_experts, 1, f2)
    b2r = b2.reshape(b2.shape[0], n_experts, 1, d)

    n_items = nc * nf

    def expert_and_tile(item, ce, cn):
        c = item // nf
        return ce[c], jnp.where(cn[c] > 0, item % nf, nf - 1)

    def weights_of_next(s, ce, cn):
        return expert_and_tile(jnp.minimum(s, n_items - 1), ce, cn)

    def biases_of_current(s, ce, cn):
        return expert_and_tile(jnp.maximum(s - 1, 0), ce, cn)

    def w1_map(half):
        def index(s, ce, cr, cn):
            e, f = weights_of_next(s, ce, cn)
            return layer, e, 0, half * nf + f
        return index

    def b1_map(half):
        def index(s, ce, cr, cn):
            e, f = biases_of_current(s, ce, cn)
            return layer, e, 0, half * nf + f
        return index

    def w2_map(s, ce, cr, cn):
        e, f = weights_of_next(s, ce, cn)
        return layer, e, f, 0

    def b2_map(s, ce, cr, cn):
        e, _ = biases_of_current(s, ce, cn)
        return layer, e, 0, 0

    assert nf >= 2, "the next chunk's rows are requested during a chunk's second d_ff tile"
    kern = functools.partial(_expert_kernel, nf=nf, n_chunks=nc)
    return pl.pallas_call(
        kern,
        grid_spec=pltpu.PrefetchScalarGridSpec(
            num_scalar_prefetch=3,
            grid=(n_items + 1,),
            in_specs=[
                pl.BlockSpec(memory_space=pl.ANY),
                pl.BlockSpec((None, None, d, tf), w1_map(0)),
                pl.BlockSpec((None, None, d, tf), w1_map(1)),
                pl.BlockSpec((None, None, 1, tf), b1_map(0)),
                pl.BlockSpec((None, None, 1, tf), b1_map(1)),
                pl.BlockSpec((None, None, tf, d), w2_map),
                pl.BlockSpec((None, None, 1, d), b2_map),
            ],
            out_specs=pl.BlockSpec(memory_space=pl.ANY),
            scratch_shapes=[
                pltpu.VMEM((CHUNK_ROWS, d), BF16),
                pltpu.VMEM((CHUNK_ROWS, d), F32),
                pltpu.VMEM((CHUNK_ROWS, d), F32),
                pltpu.VMEM((2, d, tf), BF16),
                pltpu.VMEM((2, d, tf), BF16),
                pltpu.VMEM((2, tf, d), BF16),
                pltpu.SemaphoreType.DMA((2,)),
            ],
        ),
        out_shape=jax.ShapeDtypeStruct((p, d), F32),
        input_output_aliases={3: 0},
        compiler_params=_cparams(("arbitrary",)),
        name="expert_ffn",
    )(chunk_e, chunk_row, chunk_nv, x_buf, w1, w1, b1r, b1r, w2, b2r)


def _combine_kernel(dest_ref, y_hbm, gate_ref, h_ref, g_ref, b_ref, out_ref, buf_ref, sem, *, tm, alpha):
    step = pl.program_id(0)
    slot = step % 2

    def gather(tile, into):
        def issue(t, carry):
            for k in range(TOP_K):
                r = dest_ref[(tile * tm + t) * TOP_K + k]
                pltpu.make_async_copy(y_hbm.at[pl.ds(r, 1), :], buf_ref.at[into, k, pl.ds(t, 1), :],
                                      sem.at[into]).start()
            return carry
        lax.fori_loop(0, tm, issue, 0, unroll=DMA_ISSUE_UNROLL)

    @pl.when(step == 0)
    def _():
        gather(0, 0)

    @pl.when(step + 1 < pl.num_programs(0))
    def _():
        gather(step + 1, 1 - slot)

    for k in range(TOP_K):
        pltpu.make_async_copy(y_hbm.at[pl.ds(0, tm), :], buf_ref.at[slot, k], sem.at[slot]).wait()

    gates = gate_ref[...]
    y = gates[:, 0:1] * buf_ref[slot, 0]
    for k in range(1, TOP_K):
        y = y + gates[:, k:k + 1] * buf_ref[slot, k]
    out_ref[...] = _layer_norm_rows(alpha * h_ref[...] + y, g_ref[...], b_ref[...])


def _combine_ln(dest_flat, y_buf, gates, h2d, g, b, *, alpha, tm):
    t, d = h2d.shape
    kern = functools.partial(_combine_kernel, tm=tm, alpha=alpha)
    return pl.pallas_call(
        kern,
        grid_spec=pltpu.PrefetchScalarGridSpec(
            num_scalar_prefetch=1,
            grid=(t // tm,),
            in_specs=[
                pl.BlockSpec(memory_space=pl.ANY),
                pl.BlockSpec((tm, LANES), lambda i, dest: (i, 0)),
                pl.BlockSpec((tm, d), lambda i, dest: (i, 0)),
                pl.BlockSpec((1, d), lambda i, dest: (0, 0)),
                pl.BlockSpec((1, d), lambda i, dest: (0, 0)),
            ],
            out_specs=pl.BlockSpec((tm, d), lambda i, dest: (i, 0)),
            scratch_shapes=[pltpu.VMEM((2, TOP_K, tm, d), F32), pltpu.SemaphoreType.DMA((2,))],
        ),
        out_shape=jax.ShapeDtypeStruct((t, d), F32),
        compiler_params=_cparams(("arbitrary",)),
        name="combine_ln",
    )(dest_flat, y_buf, gates, h2d, g, b)


def _routing_tables(idx, rank, counts, *, n_chunks):
    padded = (counts + SUBLANES - 1) // SUBLANES * SUBLANES
    pad_end = jnp.cumsum(padded)
    pad_start = pad_end - padded
    dest = pad_start[idx] + rank
    nchunk = (padded + CHUNK_ROWS - 1) // CHUNK_ROWS
    chunk_end = jnp.cumsum(nchunk)
    chunk_start = chunk_end - nchunk
    total = chunk_end[-1]
    cids = jnp.arange(n_chunks, dtype=jnp.int32)
    clamped = jnp.minimum(cids, total - 1)
    ce = jnp.sum((chunk_end[None, :] <= clamped[:, None]).astype(jnp.int32), axis=1)
    local = clamped - chunk_start[ce]
    crow = (pad_start[ce] + local * CHUNK_ROWS).astype(jnp.int32)
    cnv = jnp.clip(padded[ce] - local * CHUNK_ROWS, 0, CHUNK_ROWS)
    cnv = jnp.where(cids < total, cnv, 0).astype(jnp.int32)
    fill = counts[:, None] + jnp.arange(SUBLANES, dtype=jnp.int32)[None, :]
    pad_rows = jnp.where(fill < padded[:, None], pad_start[:, None] + fill, -1)
    return (dest.reshape(-1).astype(jnp.int32), pad_rows.reshape(-1).astype(jnp.int32),
            ce.astype(jnp.int32), crow, cnv)


def _moe_ln(h2d, router_w, router_b, w1, b1, w2, b2, g, b, *, layer, alpha, tiles):
    t, d = h2d.shape
    n_experts = router_w.shape[1]
    w_pad = jnp.pad(router_w, ((0, 0), (0, LANES - n_experts)))
    b_pad = jnp.pad(router_b, (0, LANES - n_experts)).reshape(1, LANES)
    idx, gates, rank, counts = _router(h2d, w_pad, b_pad, n_experts=n_experts, tm=tiles["router_tm"])
    n_chunks = (t * TOP_K + n_experts * SUBLANES) // CHUNK_ROWS + n_experts
    dest_flat, pad_rows, ce, crow, cnv = _routing_tables(idx[:, :TOP_K], rank[:, :TOP_K],
                                                         counts[0, :n_experts], n_chunks=n_chunks)
    p_rows = t * TOP_K + n_experts * SUBLANES + CHUNK_ROWS
    x_buf = _dispatch(dest_flat, pad_rows, h2d, p_rows=p_rows, tm=tiles["dispatch_tm"])
    y_buf = _expert_ffn(ce, crow, cnv, x_buf, w1, b1, w2, b2, layer=layer, tf=tiles["expert_tf"])
    return _combine_ln(dest_flat, y_buf, gates, h2d, g, b, alpha=alpha, tm=tiles["combine_tm"])


def _rope_tables(positions, q_scale):
    inv_freq = ROPE_THETA ** (-jnp.arange(0, HEAD_DIM, 2, dtype=F32) / HEAD_DIM)
    ang = positions.astype(F32).reshape(-1)[:, None] * inv_freq
    cos, sin = jnp.cos(ang), jnp.sin(ang)
    cos2 = jnp.concatenate([cos, cos], axis=-1)
    sin2 = jnp.concatenate([-sin, sin], axis=-1)
    k_tab = jnp.stack([cos2, sin2])
    return jnp.stack([k_tab * q_scale, k_tab])


def _tiles(t, s, d, d_ff):
    return {
        "inproj_tm": min(1024, t), "inproj_tn": min(1024, d),
        "da_tq": min(512, s), "da_tk": min(512, s),
        "sb_tq": min(512, s), "sb_tk": min(256, s), "sb_heads_per_step": min(4, d // HEAD_DIM),
        "outproj_tm": min(256, t),
        "router_tm": min(256, t),
        "dispatch_tm": min(256, t),
        "combine_tm": min(128, t),
        "expert_tf": min(256, d_ff),
    }


def kernel(x, positions, da_w_in, da_lambda_q1, da_lambda_k1, da_lambda_q2, da_lambda_k2, da_subln_g, da_w_out,
           sb_w_in, sb_w_out, ln1_g, ln1_b, ln2_g, ln2_b, router_w, router_b, expert_w1, expert_b1, expert_w2,
           expert_b2):
    bsz, seq, d = x.shape
    t = bsz * seq
    depth = ln1_g.shape[0]
    d_ff = expert_w2.shape[2]
    alpha = (2 * depth) ** 0.25
    q_scale = HEAD_DIM ** -0.5 * math.log2(math.e)
    tiles = _tiles(t, seq, d, d_ff)
    tab = _rope_tables(positions, q_scale)

    h = x.reshape(t, d)
    for i in range(depth):
        j = i // 2
        if i % 2 == 0:
            lambda_init = 0.8 - 0.6 * math.exp(-0.3 * i)
            proj = _in_projection(h, da_w_in[j].astype(BF16), tab, rope=True, d_model=d, q_scale=q_scale,
                                  tm=tiles["inproj_tm"], tn=tiles["inproj_tn"])
            lam_rows = jnp.stack([da_lambda_q1[j], da_lambda_k1[j], da_lambda_q2[j], da_lambda_k2[j]])
            o = _diff_attention(proj.reshape(bsz, seq, 3 * d), lam_rows, da_subln_g[j].reshape(1, DA_V_DIM),
                                lambda_init=lambda_init, tq=tiles["da_tq"], tk=tiles["da_tk"])
            w_out = da_w_out[j]
        else:
            proj = _in_projection(h, sb_w_in[j].astype(BF16), tab, rope=False, d_model=d, q_scale=q_scale,
                                  tm=tiles["inproj_tm"], tn=tiles["inproj_tn"])
            o = _stick_breaking_attention(proj.reshape(bsz, seq, 3 * d), tq=tiles["sb_tq"], tk=tiles["sb_tk"],
                                          hp=tiles["sb_heads_per_step"])
            w_out = sb_w_out[j]
        h = _out_projection_ln(o.reshape(t, d), w_out.astype(BF16), h, ln1_g[i].reshape(1, d),
                               ln1_b[i].reshape(1, d), alpha=alpha, tm=tiles["outproj_tm"])
        h = _moe_ln(h, router_w[i], router_b[i], expert_w1, expert_b1, expert_w2, expert_b2,
                    ln2_g[i].reshape(1, d), ln2_b[i].reshape(1, d), layer=i, alpha=alpha, tiles=tiles)
    return h.reshape(bsz, seq, d)
```

```python
import functools
import math

import jax
import jax.numpy as jnp
from jax import lax
from jax.experimental import pallas as pl
from jax.experimental.pallas import tpu as pltpu

F32 = jnp.float32
BF16 = jnp.bfloat16

HEAD_DIM = 128
DA_V_DIM = 2 * HEAD_DIM
TOP_K = 4
SWIGLU_ALPHA = 1.702
SWIGLU_LIMIT = 7.0
ROPE_THETA = 10000.0
LN_EPS = 1e-5

LANES = 128
SUBLANES = 8
VMEM_LIMIT_BYTES = 56 * 1024 * 1024

CHUNK_ROWS = 1088
SMALL_ROWS = 128
PIPE_ROWS = CHUNK_ROWS
ZERO_ROWS = 64
DMA_ISSUE_UNROLL = 8


def _cparams(sem):
    return pltpu.CompilerParams(dimension_semantics=sem, vmem_limit_bytes=VMEM_LIMIT_BYTES)


def _layer_norm_rows(z, g, b):
    mu = jnp.mean(z, axis=-1, keepdims=True)
    zc = z - mu
    var = jnp.mean(zc * zc, axis=-1, keepdims=True)
    return zc * lax.rsqrt(var + LN_EPS) * g + b


def _inproj_kernel(x_ref, w_ref, tab_ref, o_ref, xb_ref, *, rope, qk_tiles, q_tiles, q_scale):
    j = pl.program_id(1)

    @pl.when(j == 0)
    def _():
        xb_ref[...] = x_ref[...].astype(BF16)

    acc = jnp.dot(xb_ref[...], w_ref[...], preferred_element_type=F32)
    tn = acc.shape[1]

    if rope:
        @pl.when(j < qk_tiles)
        def _():
            cos = tab_ref[0]
            sin = tab_ref[1]
            for c in range(tn // LANES):
                xc = acc[:, c * LANES:(c + 1) * LANES]
                rot = pltpu.roll(xc, LANES // 2, 1)
                o_ref[:, c * LANES:(c + 1) * LANES] = (xc * cos + rot * sin).astype(o_ref.dtype)

        @pl.when(j >= qk_tiles)
        def _():
            o_ref[...] = acc.astype(o_ref.dtype)
    else:
        @pl.when(j < q_tiles)
        def _():
            o_ref[...] = (acc * q_scale).astype(o_ref.dtype)

        @pl.when(j >= q_tiles)
        def _():
            o_ref[...] = acc.astype(o_ref.dtype)


def _in_projection(x2d, w_bf16, tab, *, rope, d_model, q_scale, tm, tn):
    m, k = x2d.shape
    n = w_bf16.shape[1]
    q_tiles = d_model // tn
    qk_tiles = 2 * q_tiles
    kern = functools.partial(_inproj_kernel, rope=rope, qk_tiles=qk_tiles, q_tiles=q_tiles, q_scale=q_scale)
    return pl.pallas_call(
        kern,
        grid=(m // tm, n // tn),
        in_specs=[
            pl.BlockSpec((tm, k), lambda i, j: (i, 0)),
            pl.BlockSpec((k, tn), lambda i, j: (0, j)),
            pl.BlockSpec((None, 2, tm, LANES), lambda i, j: (jnp.minimum(j // q_tiles, 1), 0, i, 0)),
        ],
        out_specs=pl.BlockSpec((tm, tn), lambda i, j: (i, j)),
        out_shape=jax.ShapeDtypeStruct((m, n), BF16),
        scratch_shapes=[pltpu.VMEM((tm, k), BF16)],
        compiler_params=_cparams(("parallel", "arbitrary")),
        name="in_projection",
    )(x2d, w_bf16, tab)


def _transpose_bf16(x):
    return x.astype(F32).T.astype(BF16)


def _store_v_transposed(v_ref, vt_ref, tk):
    for n in range(vt_ref.shape[0]):
        vt_ref[n] = _transpose_bf16(v_ref[n * tk:(n + 1) * tk, :])


def _da_kernel(q_ref, k_ref, v_ref, lam_ref, g_ref, o_ref, vt_ref, acc_ref, m_ref, l_ref, *, tq, tk, lambda_init):
    i = pl.program_id(2)
    diag_tiles = tq // tk

    @pl.when(i == 0)
    def _():
        _store_v_transposed(v_ref, vt_ref, tk)

    m_ref[...] = jnp.full(m_ref.shape, -jnp.inf, F32)
    l_ref[...] = jnp.zeros(l_ref.shape, F32)
    acc_ref[...] = jnp.zeros(acc_ref.shape, F32)
    q_t = _transpose_bf16(q_ref[...])
    key = lax.broadcasted_iota(jnp.int32, (tk, tq), 0)
    qry = lax.broadcasted_iota(jnp.int32, (tk, tq), 1)

    def step(j, shift):
        off = pl.multiple_of(j * tk, tk)
        ks = k_ref[pl.ds(off, tk), :]
        v_t = vt_ref[j]
        comps = range(2)
        s = [jnp.dot(ks[:, c * HEAD_DIM:(c + 1) * HEAD_DIM], q_t[c * HEAD_DIM:(c + 1) * HEAD_DIM, :],
                     preferred_element_type=F32) for c in comps]
        if shift is not None:
            visible = key + shift <= qry
            s = [jnp.where(visible, sc, -jnp.inf) for sc in s]
        m_old = [m_ref[c] for c in comps]
        m_new = [jnp.maximum(m_old[c], jnp.max(s[c], axis=0, keepdims=True)) for c in comps]
        alpha = [jnp.exp2(m_old[c] - m_new[c]) for c in comps]
        p = [jnp.exp2(s[c] - m_new[c]) for c in comps]
        for c in comps:
            l_ref[c] = alpha[c] * l_ref[c] + jnp.sum(p[c], axis=0, keepdims=True)
            m_ref[c] = m_new[c]
        pv = [jnp.dot(v_t, p[c].astype(BF16), preferred_element_type=F32) for c in comps]
        for c in comps:
            acc_ref[c] = alpha[c] * acc_ref[c] + pv[c]

    def body(j, carry):
        step(j, None)
        return carry

    lax.fori_loop(0, i * diag_tiles, body, 0)
    for dj in range(diag_tiles):
        step(i * diag_tiles + dj, dj * tk)

    lam_rows = lam_ref[...]
    lam = (jnp.exp(jnp.sum(lam_rows[0:1] * lam_rows[1:2], axis=-1, keepdims=True))
           - jnp.exp(jnp.sum(lam_rows[2:3] * lam_rows[3:4], axis=-1, keepdims=True)) + lambda_init)
    o_t = acc_ref[0] / l_ref[0] - lam * (acc_ref[1] / l_ref[1])
    o = o_t.T
    o = o * lax.rsqrt(jnp.mean(o * o, axis=-1, keepdims=True) + LN_EPS) * g_ref[...]
    o_ref[...] = (o * (1.0 - lambda_init)).astype(o_ref.dtype)


def _diff_attention(proj, lam_rows, subln_g, *, lambda_init, tq, tk):
    b, s, n3 = proj.shape
    d_model = n3 // 3
    heads = d_model // DA_V_DIM
    kern = functools.partial(_da_kernel, tq=tq, tk=tk, lambda_init=lambda_init)
    return pl.pallas_call(
        kern,
        grid=(b, heads, s // tq),
        in_specs=[
            pl.BlockSpec((None, tq, DA_V_DIM), lambda bi, h, i: (bi, i, h)),
            pl.BlockSpec((None, s, DA_V_DIM), lambda bi, h, i: (bi, 0, heads + h)),
            pl.BlockSpec((None, s, DA_V_DIM), lambda bi, h, i: (bi, 0, 2 * heads + h)),
            pl.BlockSpec((4, HEAD_DIM), lambda bi, h, i: (0, 0)),
            pl.BlockSpec((1, DA_V_DIM), lambda bi, h, i: (0, 0)),
        ],
        out_specs=pl.BlockSpec((None, tq, DA_V_DIM), lambda bi, h, i: (bi, i, h)),
        out_shape=jax.ShapeDtypeStruct((b, s, d_model), BF16),
        scratch_shapes=[pltpu.VMEM((s // tk, DA_V_DIM, tk), BF16), pltpu.VMEM((2, DA_V_DIM, tq), F32),
                        pltpu.VMEM((2, 1, tq), F32), pltpu.VMEM((2, 1, tq), F32)],
        compiler_params=_cparams(("parallel", "parallel", "arbitrary")),
        name="diff_attention",
    )(proj, proj, proj, lam_rows, subln_g)


def _sb_kernel(q_ref, k_ref, v_ref, o_ref, vt_ref, acc_ref, run_ref, *, tq, tk, hp):
    i = pl.program_id(2)
    diag_tiles = tq // tk
    heads = range(hp)

    def head(x, h, axis):
        return x[:, h * HEAD_DIM:(h + 1) * HEAD_DIM] if axis == 1 else x[h * HEAD_DIM:(h + 1) * HEAD_DIM, :]

    @pl.when(i == 0)
    def _():
        _store_v_transposed(v_ref, vt_ref, tk)

    acc_ref[...] = jnp.zeros(acc_ref.shape, F32)
    run_ref[...] = jnp.zeros(run_ref.shape, F32)
    q_t = _transpose_bf16(q_ref[...])
    key = lax.broadcasted_iota(jnp.int32, (tk, tq), 0)
    qry = lax.broadcasted_iota(jnp.int32, (tk, tq), 1)
    k_a = lax.broadcasted_iota(jnp.int32, (tk, tk), 0)
    k_b = lax.broadcasted_iota(jnp.int32, (tk, tk), 1)
    later = (k_b > k_a).astype(BF16)

    def step(j, shift):
        off = pl.multiple_of(j * tk, tk)
        ks = k_ref[pl.ds(off, tk), :]
        v_t = vt_ref[j]
        z = [jnp.dot(head(ks, h, 1), head(q_t, h, 0), preferred_element_type=F32) for h in heads]
        log_beta = [jnp.minimum(zh, 0.0) - jnp.log2(1.0 + jnp.exp2(-jnp.abs(zh))) for zh in z]
        log_1m = [log_beta[h] - z[h] for h in heads]
        if shift is not None:
            strict = key + shift < qry
            log_1m = [jnp.where(strict, x, 0.0) for x in log_1m]
        after = [jnp.dot(later, x.astype(BF16), preferred_element_type=F32) for x in log_1m]
        run = [run_ref[h] for h in heads]
        a = [jnp.exp2(log_beta[h] + after[h] + run[h]) for h in heads]
        if shift is not None:
            a = [jnp.where(strict, x, 0.0) for x in a]
        for h in heads:
            acc_ref[h] += jnp.dot(head(v_t, h, 0), a[h].astype(BF16), preferred_element_type=F32)
            run_ref[h] = run[h] + after[h][0:1, :] + log_1m[h][0:1, :]

    for dj in reversed(range(diag_tiles)):
        step(i * diag_tiles + dj, dj * tk)

    def body(jj, carry):
        step(i * diag_tiles - 1 - jj, None)
        return carry

    lax.fori_loop(0, i * diag_tiles, body, 0)
    for h in heads:
        o_ref[:, h * HEAD_DIM:(h + 1) * HEAD_DIM] = acc_ref[h].T.astype(o_ref.dtype)


def _stick_breaking_attention(proj, *, tq, tk, hp):
    b, s, n3 = proj.shape
    d_model = n3 // 3
    heads = d_model // HEAD_DIM
    groups = heads // hp
    width = hp * HEAD_DIM
    kern = functools.partial(_sb_kernel, tq=tq, tk=tk, hp=hp)
    return pl.pallas_call(
        kern,
        grid=(b, groups, s // tq),
        in_specs=[
            pl.BlockSpec((None, tq, width), lambda bi, h, i: (bi, i, h)),
            pl.BlockSpec((None, s, width), lambda bi, h, i: (bi, 0, groups + h)),
            pl.BlockSpec((None, s, width), lambda bi, h, i: (bi, 0, 2 * groups + h)),
        ],
        out_specs=pl.BlockSpec((None, tq, width), lambda bi, h, i: (bi, i, h)),
        out_shape=jax.ShapeDtypeStruct((b, s, d_model), BF16),
        scratch_shapes=[pltpu.VMEM((s // tk, width, tk), BF16), pltpu.VMEM((hp, HEAD_DIM, tq), F32),
                        pltpu.VMEM((hp, 1, tq), F32)],
        compiler_params=_cparams(("parallel", "parallel", "arbitrary")),
        name="stick_breaking_attention",
    )(proj, proj, proj)


def _outproj_ln_kernel(o_ref, w_ref, h_ref, g_ref, b_ref, out_ref, *, alpha):
    mix = jnp.dot(o_ref[...], w_ref[...], preferred_element_type=F32)
    out_ref[...] = _layer_norm_rows(alpha * h_ref[...] + mix, g_ref[...], b_ref[...])


def _out_projection_ln(o2d, w_bf16, h2d, g, b, *, alpha, tm):
    m, d = h2d.shape
    kern = functools.partial(_outproj_ln_kernel, alpha=alpha)
    return pl.pallas_call(
        kern,
        grid=(m // tm,),
        in_specs=[
            pl.BlockSpec((tm, d), lambda i: (i, 0)),
            pl.BlockSpec((d, d), lambda i: (0, 0)),
            pl.BlockSpec((tm, d), lambda i: (i, 0)),
            pl.BlockSpec((1, d), lambda i: (0, 0)),
            pl.BlockSpec((1, d), lambda i: (0, 0)),
        ],
        out_specs=pl.BlockSpec((tm, d), lambda i: (i, 0)),
        out_shape=jax.ShapeDtypeStruct((m, d), F32),
        compiler_params=_cparams(("parallel",)),
        name="out_projection_ln",
    )(o2d, w_bf16, h2d, g, b)


def _router_kernel(h_ref, w_ref, b_ref, idx_ref, gate_ref, rank_ref, cnt_ref, carry_ref, *, n_experts):
    step = pl.program_id(0)

    @pl.when(step == 0)
    def _():
        carry_ref[...] = jnp.zeros(carry_ref.shape, F32)

    tm = h_ref.shape[0]
    logits = jnp.dot(h_ref[...], w_ref[...], preferred_element_type=F32,
                     precision=lax.Precision.HIGHEST) + b_ref[...]
    lane = lax.broadcasted_iota(jnp.int32, (tm, LANES), 1)
    work = jnp.where(lane < n_experts, logits, -jnp.inf)

    vals, idxs = [], []
    onehot = jnp.zeros((tm, LANES), F32)
    for _ in range(TOP_K):
        mx = jnp.max(work, axis=-1, keepdims=True)
        ix = jnp.min(jnp.where(work == mx, lane, LANES), axis=-1, keepdims=True)
        sel = lane == ix
        vals.append(mx)
        idxs.append(ix)
        onehot = jnp.where(sel, 1.0, onehot)
        work = jnp.where(sel, -jnp.inf, work)

    exps = [jnp.exp(v - vals[0]) for v in vals]
    denom = exps[0]
    for e in exps[1:]:
        denom = denom + e

    r_i = lax.broadcasted_iota(jnp.int32, (tm, tm), 0)
    c_i = lax.broadcasted_iota(jnp.int32, (tm, tm), 1)
    lower = (c_i < r_i).astype(BF16)
    earlier = jnp.dot(lower, onehot.astype(BF16), preferred_element_type=F32) + carry_ref[...]

    idx_out = jnp.zeros((tm, LANES), jnp.int32)
    gate_out = jnp.zeros((tm, LANES), F32)
    rank_out = jnp.zeros((tm, LANES), jnp.int32)
    for k in range(TOP_K):
        rk = jnp.sum(jnp.where(lane == idxs[k], earlier, 0.0), axis=-1, keepdims=True)
        idx_out = jnp.where(lane == k, idxs[k], idx_out)
        gate_out = jnp.where(lane == k, exps[k] / denom, gate_out)
        rank_out = jnp.where(lane == k, rk.astype(jnp.int32), rank_out)
    idx_ref[...] = idx_out
    gate_ref[...] = gate_out
    rank_ref[...] = rank_out

    total = carry_ref[...] + jnp.sum(onehot, axis=0, keepdims=True)
    carry_ref[...] = total
    cnt_ref[...] = total.astype(jnp.int32)


def _router(h2d, w_pad, b_pad, *, n_experts, tm):
    t, d = h2d.shape
    kern = functools.partial(_router_kernel, n_experts=n_experts)
    tok_spec = pl.BlockSpec((tm, LANES), lambda i: (i, 0))
    return pl.pallas_call(
        kern,
        grid=(t // tm,),
        in_specs=[
            pl.BlockSpec((tm, d), lambda i: (i, 0)),
            pl.BlockSpec((d, LANES), lambda i: (0, 0)),
            pl.BlockSpec((1, LANES), lambda i: (0, 0)),
        ],
        out_specs=[tok_spec, tok_spec, tok_spec, pl.BlockSpec((1, LANES), lambda i: (0, 0))],
        out_shape=[jax.ShapeDtypeStruct((t, LANES), jnp.int32), jax.ShapeDtypeStruct((t, LANES), F32),
                   jax.ShapeDtypeStruct((t, LANES), jnp.int32), jax.ShapeDtypeStruct((1, LANES), jnp.int32)],
        scratch_shapes=[pltpu.VMEM((1, LANES), F32)],
        compiler_params=_cparams(("arbitrary",)),
        name="router",
    )(h2d, w_pad, b_pad)


def _dispatch_kernel(dest_ref, pad_ref, h_ref, x_hbm, zero_ref, sem, *, tm, tail_start, tail_rows):
    step = pl.program_id(0)
    base = step * tm

    @pl.when(step == 0)
    def _():
        zero_ref[...] = jnp.zeros(zero_ref.shape, F32)

        def tail_copy(n):
            return pltpu.make_async_copy(zero_ref, x_hbm.at[pl.ds(tail_start + n * ZERO_ROWS, ZERO_ROWS), :], sem)

        for n in range(tail_rows // ZERO_ROWS):
            tail_copy(n).start()
        for n in range(tail_rows // ZERO_ROWS):
            tail_copy(n).wait()

        def pad_copy(i):
            return pltpu.make_async_copy(h_ref.at[pl.ds(0, 1), :], x_hbm.at[pl.ds(pad_ref[i], 1), :], sem)

        def pad_start(i, carry):
            @pl.when(pad_ref[i] >= 0)
            def _():
                pad_copy(i).start()
            return carry

        def pad_wait(i, carry):
            @pl.when(pad_ref[i] >= 0)
            def _():
                pad_copy(i).wait()
            return carry

        lax.fori_loop(0, pad_ref.shape[0], pad_start, 0)
        lax.fori_loop(0, pad_ref.shape[0], pad_wait, 0)

    def row_copy(t, k):
        r = dest_ref[(base + t) * TOP_K + k]
        return pltpu.make_async_copy(h_ref.at[pl.ds(t, 1), :], x_hbm.at[pl.ds(r, 1), :], sem)

    def issue(t, carry):
        for k in range(TOP_K):
            row_copy(t, k).start()
        return carry

    lax.fori_loop(0, tm, issue, 0, unroll=DMA_ISSUE_UNROLL)
    for _ in range(TOP_K):
        pltpu.make_async_copy(h_ref, x_hbm.at[pl.ds(0, tm), :], sem).wait()


def _dispatch(dest_flat, pad_rows, h2d, *, p_rows, tm):
    t, d = h2d.shape
    tail_start = t * TOP_K
    tail_rows = p_rows - tail_start
    assert tail_rows % ZERO_ROWS == 0
    kern = functools.partial(_dispatch_kernel, tm=tm, tail_start=tail_start, tail_rows=tail_rows)
    return pl.pallas_call(
        kern,
        grid_spec=pltpu.PrefetchScalarGridSpec(
            num_scalar_prefetch=2,
            grid=(t // tm,),
            in_specs=[pl.BlockSpec((tm, d), lambda i, dest, pad: (i, 0))],
            out_specs=pl.BlockSpec(memory_space=pl.ANY),
            scratch_shapes=[pltpu.VMEM((ZERO_ROWS, d), F32), pltpu.SemaphoreType.DMA(())],
        ),
        out_shape=jax.ShapeDtypeStruct((p_rows, d), F32),
        compiler_params=_cparams(("arbitrary",)),
        name="dispatch",
    )(dest_flat, pad_rows, h2d)


def _expert_kernel(ce_ref, cr_ref, cn_ref, x_hbm, w1g_ref, w1l_ref, b1g_ref, b1l_ref, w2_ref, b2_ref, y_hbm,
                   xb_ref, y_ref, stage_ref, w1g_b, w1l_b, w2_b, sem, *, nf, n_chunks):
    del ce_ref
    s = pl.program_id(0)
    item = jnp.maximum(s - 1, 0)
    c = item // nf
    f = item % nf
    nv = jnp.where(s > 0, cn_ref[c], 0)
    d = y_ref.shape[2]
    yslot = c % 2
    cur = (s + 1) % 2
    nxt = s % 2

    def cast_next_weights():
        w1g_b[nxt] = w1g_ref[...].astype(BF16)
        w1l_b[nxt] = w1l_ref[...].astype(BF16)
        w2_b[nxt] = w2_ref[...].astype(BF16)

    @pl.when(s == 0)
    def _():
        cast_next_weights()

    def x_copy(chunk):
        start = pl.multiple_of(cr_ref[chunk], SUBLANES)
        return pltpu.make_async_copy(x_hbm.at[pl.ds(start, CHUNK_ROWS), :], stage_ref, sem.at[0])

    @pl.when((f == 0) & (nv > 0) & (c == 0))
    def _():
        x_copy(c).start()

    @pl.when((f == 0) & (nv > 0))
    def _():
        x_copy(c).wait()
        xb_ref[...] = stage_ref[...].astype(BF16)
        y_ref[yslot] = jnp.broadcast_to(b2_ref[...], y_ref.shape[1:])

    c_next = jnp.minimum(c + 1, n_chunks - 1)

    @pl.when((f == 1) & (nv > 0) & (c + 1 < n_chunks) & (cn_ref[c_next] > 0))
    def _():
        x_copy(c_next).start()

    def compute(rows):
        cast_next_weights()
        blk = PIPE_ROWS if rows % PIPE_ROWS == 0 else rows
        col = min(512, d)

        def first(r0):
            xs = xb_ref[r0:r0 + blk, :]
            hg = jnp.dot(xs, w1g_b[cur], preferred_element_type=F32) + b1g_ref[...]
            hl = jnp.dot(xs, w1l_b[cur], preferred_element_type=F32) + b1l_ref[...]
            return hg, hl

        def activate(hg, hl):
            glu = jnp.minimum(hg, SWIGLU_LIMIT)
            lin = jnp.clip(hl, -SWIGLU_LIMIT, SWIGLU_LIMIT)
            return (glu * jax.nn.sigmoid(SWIGLU_ALPHA * glu) * (lin + 1.0)).astype(BF16)

        def second(r0, act):
            for n in range(d // col):
                y_ref[yslot, r0:r0 + blk, n * col:(n + 1) * col] += jnp.dot(
                    act, w2_b[cur, :, n * col:(n + 1) * col], preferred_element_type=F32)

        starts = list(range(0, rows, blk))
        h = first(starts[0])
        for k, r0 in enumerate(starts):
            act = activate(*h)
            if k + 1 < len(starts):
                h = first(starts[k + 1])
            second(r0, act)

    @pl.when(nv > SMALL_ROWS)
    def _():
        compute(CHUNK_ROWS)

    @pl.when((nv > 0) & (nv <= SMALL_ROWS))
    def _():
        compute(SMALL_ROWS)

    sizes = []
    size = SUBLANES
    while size <= CHUNK_ROWS:
        sizes.append(size)
        size *= 2

    def writeback(chunk, action):
        rows_valid = cn_ref[chunk]
        start = pl.multiple_of(cr_ref[chunk], SUBLANES)
        slot = chunk % 2
        for size in sizes:
            @pl.when((rows_valid & size) != 0)
            def _():
                off = pl.multiple_of(rows_valid - (rows_valid % (2 * size)), SUBLANES)
                cp = pltpu.make_async_copy(y_ref.at[slot, pl.ds(off, size), :],
                                           y_hbm.at[pl.ds(start + off, size), :], sem.at[1 + slot])
                if action == "start":
                    cp.start()
                else:
                    cp.wait()

    last_step_of_chunk = (f == nf - 1) & (nv > 0)

    @pl.when(last_step_of_chunk & (c > 0))
    def _():
        writeback(jnp.maximum(c - 1, 0), "wait")

    @pl.when(last_step_of_chunk)
    def _():
        writeback(c, "start")

    @pl.when(last_step_of_chunk & ((c + 1 >= n_chunks) | (cn_ref[c_next] == 0)))
    def _():
        writeback(c, "wait")


def _expert_ffn(chunk_e, chunk_row, chunk_nv, n_steps, x_buf, w1, b1, w2, b2, *, layer, tf):
    p, d = x_buf.shape
    n_experts, _, f2 = w1.shape[1:]
    d_ff = f2 // 2
    nf = d_ff // tf
    nc = chunk_e.shape[0]
    b1r = b1.reshape(b1.shape[0], n_experts, 1, f2)
    b2r = b2.reshape(b2.shape[0], n_experts, 1, d)

    n_items = nc * nf

    def expert_and_tile(item, ce, cn):
        c = item // nf
        return ce[c], jnp.where(cn[c] > 0, item % nf, nf - 1)

    def weights_of_next(s, ce, cn):
        return expert_and_tile(jnp.minimum(s, n_items - 1), ce, cn)

    def biases_of_current(s, ce, cn):
        return expert_and_tile(jnp.maximum(s - 1, 0), ce, cn)

    def w1_map(half):
        def index(s, ce, cr, cn):
            e, f = weights_of_next(s, ce, cn)
            return layer, e, 0, half * nf + f
        return index

    def b1_map(half):
        def index(s, ce, cr, cn):
            e, f = biases_of_current(s, ce, cn)
            return layer, e, 0, half * nf + f
        return index

    def w2_map(s, ce, cr, cn):
        e, f = weights_of_next(s, ce, cn)
        return layer, e, f, 0

    def b2_map(s, ce, cr, cn):
        e, _ = biases_of_current(s, ce, cn)
        return layer, e, 0, 0

    assert nf >= 2, "the next chunk's rows are requested during a chunk's second d_ff tile"
    kern = functools.partial(_expert_kernel, nf=nf, n_chunks=nc)
    return pl.pallas_call(
        kern,
        grid_spec=pltpu.PrefetchScalarGridSpec(
            num_scalar_prefetch=3,
            grid=(n_steps,),
            in_specs=[
                pl.BlockSpec(memory_space=pl.ANY),
                pl.BlockSpec((None, None, d, tf), w1_map(0)),
                pl.BlockSpec((None, None, d, tf), w1_map(1)),
                pl.BlockSpec((None, None, 1, tf), b1_map(0)),
                pl.BlockSpec((None, None, 1, tf), b1_map(1)),
                pl.BlockSpec((None, None, tf, d), w2_map),
                pl.BlockSpec((None, None, 1, d), b2_map),
            ],
            out_specs=pl.BlockSpec(memory_space=pl.ANY),
            scratch_shapes=[
                pltpu.VMEM((CHUNK_ROWS, d), BF16),
                pltpu.VMEM((2, CHUNK_ROWS, d), F32),
                pltpu.VMEM((CHUNK_ROWS, d), F32),
                pltpu.VMEM((2, d, tf), BF16),
                pltpu.VMEM((2, d, tf), BF16),
                pltpu.VMEM((2, tf, d), BF16),
                pltpu.SemaphoreType.DMA((3,)),
            ],
        ),
        out_shape=jax.ShapeDtypeStruct((p, d), F32),
        input_output_aliases={3: 0},
        compiler_params=_cparams(("arbitrary",)),
        name="expert_ffn",
    )(chunk_e, chunk_row, chunk_nv, x_buf, w1, w1, b1r, b1r, w2, b2r)


def _combine_kernel(dest_ref, y_hbm, gate_ref, h_ref, g_ref, b_ref, out_ref, buf_ref, sem, *, tm, alpha):
    step = pl.program_id(0)
    slot = step % 2

    def gather(tile, into):
        def issue(t, carry):
            for k in range(TOP_K):
                r = dest_ref[(tile * tm + t) * TOP_K + k]
                pltpu.make_async_copy(y_hbm.at[pl.ds(r, 1), :], buf_ref.at[into, k, pl.ds(t, 1), :],
                                      sem.at[into]).start()
            return carry
        lax.fori_loop(0, tm, issue, 0, unroll=DMA_ISSUE_UNROLL)

    @pl.when(step == 0)
    def _():
        gather(0, 0)

    @pl.when(step + 1 < pl.num_programs(0))
    def _():
        gather(step + 1, 1 - slot)

    for k in range(TOP_K):
        pltpu.make_async_copy(y_hbm.at[pl.ds(0, tm), :], buf_ref.at[slot, k], sem.at[slot]).wait()

    gates = gate_ref[...]
    y = gates[:, 0:1] * buf_ref[slot, 0]
    for k in range(1, TOP_K):
        y = y + gates[:, k:k + 1] * buf_ref[slot, k]
    out_ref[...] = _layer_norm_rows(alpha * h_ref[...] + y, g_ref[...], b_ref[...])


def _combine_ln(dest_flat, y_buf, gates, h2d, g, b, *, alpha, tm):
    t, d = h2d.shape
    kern = functools.partial(_combine_kernel, tm=tm, alpha=alpha)
    return pl.pallas_call(
        kern,
        grid_spec=pltpu.PrefetchScalarGridSpec(
            num_scalar_prefetch=1,
            grid=(t // tm,),
            in_specs=[
                pl.BlockSpec(memory_space=pl.ANY),
                pl.BlockSpec((tm, LANES), lambda i, dest: (i, 0)),
                pl.BlockSpec((tm, d), lambda i, dest: (i, 0)),
                pl.BlockSpec((1, d), lambda i, dest: (0, 0)),
                pl.BlockSpec((1, d), lambda i, dest: (0, 0)),
            ],
            out_specs=pl.BlockSpec((tm, d), lambda i, dest: (i, 0)),
            scratch_shapes=[pltpu.VMEM((2, TOP_K, tm, d), F32), pltpu.SemaphoreType.DMA((2,))],
        ),
        out_shape=jax.ShapeDtypeStruct((t, d), F32),
        compiler_params=_cparams(("arbitrary",)),
        name="combine_ln",
    )(dest_flat, y_buf, gates, h2d, g, b)


def _routing_tables(idx, rank, counts, *, n_chunks):
    padded = (counts + SUBLANES - 1) // SUBLANES * SUBLANES
    pad_end = jnp.cumsum(padded)
    pad_start = pad_end - padded
    dest = pad_start[idx] + rank
    nchunk = (padded + CHUNK_ROWS - 1) // CHUNK_ROWS
    chunk_end = jnp.cumsum(nchunk)
    chunk_start = chunk_end - nchunk
    total = chunk_end[-1]
    cids = jnp.arange(n_chunks, dtype=jnp.int32)
    clamped = jnp.minimum(cids, total - 1)
    ce = jnp.sum((chunk_end[None, :] <= clamped[:, None]).astype(jnp.int32), axis=1)
    local = clamped - chunk_start[ce]
    crow = (pad_start[ce] + local * CHUNK_ROWS).astype(jnp.int32)
    cnv = jnp.clip(padded[ce] - local * CHUNK_ROWS, 0, CHUNK_ROWS)
    cnv = jnp.where(cids < total, cnv, 0).astype(jnp.int32)
    fill = counts[:, None] + jnp.arange(SUBLANES, dtype=jnp.int32)[None, :]
    pad_rows = jnp.where(fill < padded[:, None], pad_start[:, None] + fill, -1)
    return (dest.reshape(-1).astype(jnp.int32), pad_rows.reshape(-1).astype(jnp.int32),
            ce.astype(jnp.int32), crow, cnv)


def _moe_ln(h2d, router_w, router_b, w1, b1, w2, b2, g, b, *, layer, alpha, tiles):
    t, d = h2d.shape
    n_experts = router_w.shape[1]
    w_pad = jnp.pad(router_w, ((0, 0), (0, LANES - n_experts)))
    b_pad = jnp.pad(router_b, (0, LANES - n_experts)).reshape(1, LANES)
    idx, gates, rank, counts = _router(h2d, w_pad, b_pad, n_experts=n_experts, tm=tiles["router_tm"])
    n_chunks = (t * TOP_K + n_experts * SUBLANES) // CHUNK_ROWS + n_experts
    dest_flat, pad_rows, ce, crow, cnv = _routing_tables(idx[:, :TOP_K], rank[:, :TOP_K],
                                                         counts[0, :n_experts], n_chunks=n_chunks)
    p_rows = t * TOP_K + n_experts * SUBLANES + CHUNK_ROWS
    x_buf = _dispatch(dest_flat, pad_rows, h2d, p_rows=p_rows, tm=tiles["dispatch_tm"])
    used_chunks = jnp.sum((cnv > 0).astype(jnp.int32))
    n_steps = used_chunks * (w2.shape[2] // tiles["expert_tf"]) + 1
    y_buf = _expert_ffn(ce, crow, cnv, n_steps, x_buf, w1, b1, w2, b2, layer=layer, tf=tiles["expert_tf"])
    return _combine_ln(dest_flat, y_buf, gates, h2d, g, b, alpha=alpha, tm=tiles["combine_tm"])


def _rope_tables(positions, q_scale):
    inv_freq = ROPE_THETA ** (-jnp.arange(0, HEAD_DIM, 2, dtype=F32) / HEAD_DIM)
    ang = positions.astype(F32).reshape(-1)[:, None] * inv_freq
    cos, sin = jnp.cos(ang), jnp.sin(ang)
    cos2 = jnp.concatenate([cos, cos], axis=-1)
    sin2 = jnp.concatenate([-sin, sin], axis=-1)
    k_tab = jnp.stack([cos2, sin2])
    return jnp.stack([k_tab * q_scale, k_tab])


def _tiles(t, s, d, d_ff):
    return {
        "inproj_tm": min(1024, t), "inproj_tn": min(1024, d),
        "da_tq": min(512, s), "da_tk": min(512, s),
        "sb_tq": min(512, s), "sb_tk": min(256, s), "sb_heads_per_step": min(4, d // HEAD_DIM),
        "outproj_tm": min(512, t),
        "router_tm": min(256, t),
        "dispatch_tm": min(256, t),
        "combine_tm": min(128, t),
        "expert_tf": min(256, d_ff),
    }


def kernel(x, positions, da_w_in, da_lambda_q1, da_lambda_k1, da_lambda_q2, da_lambda_k2, da_subln_g, da_w_out,
           sb_w_in, sb_w_out, ln1_g, ln1_b, ln2_g, ln2_b, router_w, router_b, expert_w1, expert_b1, expert_w2,
           expert_b2):
    bsz, seq, d = x.shape
    t = bsz * seq
    depth = ln1_g.shape[0]
    d_ff = expert_w2.shape[2]
    alpha = (2 * depth) ** 0.25
    q_scale = HEAD_DIM ** -0.5 * math.log2(math.e)
    tiles = _tiles(t, seq, d, d_ff)
    tab = _rope_tables(positions, q_scale)

    h = x.reshape(t, d)
    for i in range(depth):
        j = i // 2
        if i % 2 == 0:
            lambda_init = 0.8 - 0.6 * math.exp(-0.3 * i)
            proj = _in_projection(h, da_w_in[j].astype(BF16), tab, rope=True, d_model=d, q_scale=q_scale,
                                  tm=tiles["inproj_tm"], tn=tiles["inproj_tn"])
            lam_rows = jnp.stack([da_lambda_q1[j], da_lambda_k1[j], da_lambda_q2[j], da_lambda_k2[j]])
            o = _diff_attention(proj.reshape(bsz, seq, 3 * d), lam_rows, da_subln_g[j].reshape(1, DA_V_DIM),
                                lambda_init=lambda_init, tq=tiles["da_tq"], tk=tiles["da_tk"])
            w_out = da_w_out[j]
        else:
            proj = _in_projection(h, sb_w_in[j].astype(BF16), tab, rope=False, d_model=d, q_scale=q_scale,
                                  tm=tiles["inproj_tm"], tn=tiles["inproj_tn"])
            o = _stick_breaking_attention(proj.reshape(bsz, seq, 3 * d), tq=tiles["sb_tq"], tk=tiles["sb_tk"],
                                          hp=tiles["sb_heads_per_step"])
            w_out = sb_w_out[j]
        h = _out_projection_ln(o.reshape(t, d), w_out.astype(BF16), h, ln1_g[i].reshape(1, d),
                               ln1_b[i].reshape(1, d), alpha=alpha, tm=tiles["outproj_tm"])
        h = _moe_ln(h, router_w[i], router_b[i], expert_w1, expert_b1, expert_w2, expert_b2,
                    ln2_g[i].reshape(1, d), ln2_b[i].reshape(1, d), layer=i, alpha=alpha, tiles=tiles)
    return h.reshape(bsz, seq, d)
```

```python
import functools
import math

import jax
import jax.numpy as jnp
from jax import lax
from jax.experimental import pallas as pl
from jax.experimental.pallas import tpu as pltpu

F32 = jnp.float32
BF16 = jnp.bfloat16

HEAD_DIM = 128
DA_V_DIM = 2 * HEAD_DIM
TOP_K = 4
SWIGLU_ALPHA = 1.702
SWIGLU_LIMIT = 7.0
ROPE_THETA = 10000.0
LN_EPS = 1e-5

LANES = 128
SUBLANES = 8
VMEM_LIMIT_BYTES = 56 * 1024 * 1024

CHUNK_ROWS = 1088
SMALL_ROWS = 128
PIPE_ROWS = CHUNK_ROWS
ZERO_ROWS = 64
DMA_ISSUE_UNROLL = 8


def _cparams(sem):
    return pltpu.CompilerParams(dimension_semantics=sem, vmem_limit_bytes=VMEM_LIMIT_BYTES)


def _layer_norm_rows(z, g, b):
    mu = jnp.mean(z, axis=-1, keepdims=True)
    zc = z - mu
    var = jnp.mean(zc * zc, axis=-1, keepdims=True)
    return zc * lax.rsqrt(var + LN_EPS) * g + b


def _inproj_kernel(x_ref, w_ref, tab_ref, o_ref, xb_ref, *, rope, qk_tiles, q_tiles, q_scale):
    j = pl.program_id(1)

    @pl.when(j == 0)
    def _():
        xb_ref[...] = x_ref[...].astype(BF16)

    acc = jnp.dot(xb_ref[...], w_ref[...], preferred_element_type=F32)
    tn = acc.shape[1]

    if rope:
        @pl.when(j < qk_tiles)
        def _():
            cos = tab_ref[0]
            sin = tab_ref[1]
            for c in range(tn // LANES):
                xc = acc[:, c * LANES:(c + 1) * LANES]
                rot = pltpu.roll(xc, LANES // 2, 1)
                o_ref[:, c * LANES:(c + 1) * LANES] = (xc * cos + rot * sin).astype(o_ref.dtype)

        @pl.when(j >= qk_tiles)
        def _():
            o_ref[...] = acc.astype(o_ref.dtype)
    else:
        @pl.when(j < q_tiles)
        def _():
            o_ref[...] = (acc * q_scale).astype(o_ref.dtype)

        @pl.when(j >= q_tiles)
        def _():
            o_ref[...] = acc.astype(o_ref.dtype)


def _in_projection(x2d, w_bf16, tab, *, rope, d_model, q_scale, tm, tn):
    m, k = x2d.shape
    n = w_bf16.shape[1]
    q_tiles = d_model // tn
    qk_tiles = 2 * q_tiles
    kern = functools.partial(_inproj_kernel, rope=rope, qk_tiles=qk_tiles, q_tiles=q_tiles, q_scale=q_scale)
    return pl.pallas_call(
        kern,
        grid=(m // tm, n // tn),
        in_specs=[
            pl.BlockSpec((tm, k), lambda i, j: (i, 0)),
            pl.BlockSpec((k, tn), lambda i, j: (0, j)),
            pl.BlockSpec((None, 2, tm, LANES), lambda i, j: (jnp.minimum(j // q_tiles, 1), 0, i, 0)),
        ],
        out_specs=pl.BlockSpec((tm, tn), lambda i, j: (i, j)),
        out_shape=jax.ShapeDtypeStruct((m, n), BF16),
        scratch_shapes=[pltpu.VMEM((tm, k), BF16)],
        compiler_params=_cparams(("parallel", "arbitrary")),
        name="in_projection",
    )(x2d, w_bf16, tab)


def _transpose_bf16(x):
    return x.astype(F32).T.astype(BF16)


def _store_v_transposed(v_ref, vt_ref, tk):
    for n in range(vt_ref.shape[0]):
        vt_ref[n] = _transpose_bf16(v_ref[n * tk:(n + 1) * tk, :])


def _da_kernel(q_ref, k_ref, v_ref, lam_ref, g_ref, o_ref, vt_ref, acc_ref, m_ref, l_ref, *, tq, tk, lambda_init):
    i = pl.program_id(2)
    diag_tiles = tq // tk

    @pl.when(i == 0)
    def _():
        _store_v_transposed(v_ref, vt_ref, tk)

    m_ref[...] = jnp.full(m_ref.shape, -jnp.inf, F32)
    l_ref[...] = jnp.zeros(l_ref.shape, F32)
    acc_ref[...] = jnp.zeros(acc_ref.shape, F32)
    q_t = _transpose_bf16(q_ref[...])
    key = lax.broadcasted_iota(jnp.int32, (tk, tq), 0)
    qry = lax.broadcasted_iota(jnp.int32, (tk, tq), 1)

    def step(j, shift):
        off = pl.multiple_of(j * tk, tk)
        ks = k_ref[pl.ds(off, tk), :]
        v_t = vt_ref[j]
        comps = range(2)
        s = [jnp.dot(ks[:, c * HEAD_DIM:(c + 1) * HEAD_DIM], q_t[c * HEAD_DIM:(c + 1) * HEAD_DIM, :],
                     preferred_element_type=F32) for c in comps]
        if shift is not None:
            visible = key + shift <= qry
            s = [jnp.where(visible, sc, -jnp.inf) for sc in s]
        m_old = [m_ref[c] for c in comps]
        m_new = [jnp.maximum(m_old[c], jnp.max(s[c], axis=0, keepdims=True)) for c in comps]
        alpha = [jnp.exp2(m_old[c] - m_new[c]) for c in comps]
        p = [jnp.exp2(s[c] - m_new[c]) for c in comps]
        for c in comps:
            l_ref[c] = alpha[c] * l_ref[c] + jnp.sum(p[c], axis=0, keepdims=True)
            m_ref[c] = m_new[c]
        pv = [jnp.dot(v_t, p[c].astype(BF16), preferred_element_type=F32) for c in comps]
        for c in comps:
            acc_ref[c] = alpha[c] * acc_ref[c] + pv[c]

    def body(j, carry):
        step(j, None)
        return carry

    lax.fori_loop(0, i * diag_tiles, body, 0)
    for dj in range(diag_tiles):
        step(i * diag_tiles + dj, dj * tk)

    lam_rows = lam_ref[...]
    lam = (jnp.exp(jnp.sum(lam_rows[0:1] * lam_rows[1:2], axis=-1, keepdims=True))
           - jnp.exp(jnp.sum(lam_rows[2:3] * lam_rows[3:4], axis=-1, keepdims=True)) + lambda_init)
    o_t = acc_ref[0] / l_ref[0] - lam * (acc_ref[1] / l_ref[1])
    o = o_t.T
    o = o * lax.rsqrt(jnp.mean(o * o, axis=-1, keepdims=True) + LN_EPS) * g_ref[...]
    o_ref[...] = (o * (1.0 - lambda_init)).astype(o_ref.dtype)


def _diff_attention(proj, lam_rows, subln_g, *, lambda_init, tq, tk):
    b, s, n3 = proj.shape
    d_model = n3 // 3
    heads = d_model // DA_V_DIM
    kern = functools.partial(_da_kernel, tq=tq, tk=tk, lambda_init=lambda_init)
    return pl.pallas_call(
        kern,
        grid=(b, heads, s // tq),
        in_specs=[
            pl.BlockSpec((None, tq, DA_V_DIM), lambda bi, h, i: (bi, i, h)),
            pl.BlockSpec((None, s, DA_V_DIM), lambda bi, h, i: (bi, 0, heads + h)),
            pl.BlockSpec((None, s, DA_V_DIM), lambda bi, h, i: (bi, 0, 2 * heads + h)),
            pl.BlockSpec((4, HEAD_DIM), lambda bi, h, i: (0, 0)),
            pl.BlockSpec((1, DA_V_DIM), lambda bi, h, i: (0, 0)),
        ],
        out_specs=pl.BlockSpec((None, tq, DA_V_DIM), lambda bi, h, i: (bi, i, h)),
        out_shape=jax.ShapeDtypeStruct((b, s, d_model), BF16),
        scratch_shapes=[pltpu.VMEM((s // tk, DA_V_DIM, tk), BF16), pltpu.VMEM((2, DA_V_DIM, tq), F32),
                        pltpu.VMEM((2, 1, tq), F32), pltpu.VMEM((2, 1, tq), F32)],
        compiler_params=_cparams(("parallel", "parallel", "arbitrary")),
        name="diff_attention",
    )(proj, proj, proj, lam_rows, subln_g)


def _sb_kernel(q_ref, k_ref, v_ref, o_ref, vt_ref, acc_ref, run_ref, *, tq, tk, hp):
    i = pl.program_id(2)
    diag_tiles = tq // tk
    heads = range(hp)

    def head(x, h, axis):
        return x[:, h * HEAD_DIM:(h + 1) * HEAD_DIM] if axis == 1 else x[h * HEAD_DIM:(h + 1) * HEAD_DIM, :]

    @pl.when(i == 0)
    def _():
        _store_v_transposed(v_ref, vt_ref, tk)

    acc_ref[...] = jnp.zeros(acc_ref.shape, F32)
    run_ref[...] = jnp.zeros(run_ref.shape, F32)
    q_t = _transpose_bf16(q_ref[...])
    key = lax.broadcasted_iota(jnp.int32, (tk, tq), 0)
    qry = lax.broadcasted_iota(jnp.int32, (tk, tq), 1)
    k_a = lax.broadcasted_iota(jnp.int32, (tk, tk), 0)
    k_b = lax.broadcasted_iota(jnp.int32, (tk, tk), 1)
    later = (k_b > k_a).astype(BF16)

    def step(j, shift):
        off = pl.multiple_of(j * tk, tk)
        ks = k_ref[pl.ds(off, tk), :]
        v_t = vt_ref[j]
        z = [jnp.dot(head(ks, h, 1), head(q_t, h, 0), preferred_element_type=F32) for h in heads]
        log_beta = [jnp.minimum(zh, 0.0) - jnp.log2(1.0 + jnp.exp2(-jnp.abs(zh))) for zh in z]
        log_1m = [log_beta[h] - z[h] for h in heads]
        if shift is not None:
            strict = key + shift < qry
            log_1m = [jnp.where(strict, x, 0.0) for x in log_1m]
        after = [jnp.dot(later, x.astype(BF16), preferred_element_type=F32) for x in log_1m]
        run = [run_ref[h] for h in heads]
        a = [jnp.exp2(log_beta[h] + after[h] + run[h]) for h in heads]
        if shift is not None:
            a = [jnp.where(strict, x, 0.0) for x in a]
        for h in heads:
            acc_ref[h] += jnp.dot(head(v_t, h, 0), a[h].astype(BF16), preferred_element_type=F32)
            run_ref[h] = run[h] + after[h][0:1, :] + log_1m[h][0:1, :]

    for dj in reversed(range(diag_tiles)):
        step(i * diag_tiles + dj, dj * tk)

    def body(jj, carry):
        step(i * diag_tiles - 1 - jj, None)
        return carry

    lax.fori_loop(0, i * diag_tiles, body, 0)
    for h in heads:
        o_ref[:, h * HEAD_DIM:(h + 1) * HEAD_DIM] = acc_ref[h].T.astype(o_ref.dtype)


def _stick_breaking_attention(proj, *, tq, tk, hp):
    b, s, n3 = proj.shape
    d_model = n3 // 3
    heads = d_model // HEAD_DIM
    groups = heads // hp
    width = hp * HEAD_DIM
    kern = functools.partial(_sb_kernel, tq=tq, tk=tk, hp=hp)
    return pl.pallas_call(
        kern,
        grid=(b, groups, s // tq),
        in_specs=[
            pl.BlockSpec((None, tq, width), lambda bi, h, i: (bi, i, h)),
            pl.BlockSpec((None, s, width), lambda bi, h, i: (bi, 0, groups + h)),
            pl.BlockSpec((None, s, width), lambda bi, h, i: (bi, 0, 2 * groups + h)),
        ],
        out_specs=pl.BlockSpec((None, tq, width), lambda bi, h, i: (bi, i, h)),
        out_shape=jax.ShapeDtypeStruct((b, s, d_model), BF16),
        scratch_shapes=[pltpu.VMEM((s // tk, width, tk), BF16), pltpu.VMEM((hp, HEAD_DIM, tq), F32),
                        pltpu.VMEM((hp, 1, tq), F32)],
        compiler_params=_cparams(("parallel", "parallel", "arbitrary")),
        name="stick_breaking_attention",
    )(proj, proj, proj)


def _outproj_ln_kernel(o_ref, w_ref, h_ref, g_ref, b_ref, out_ref, *, alpha):
    mix = jnp.dot(o_ref[...], w_ref[...], preferred_element_type=F32)
    out_ref[...] = _layer_norm_rows(alpha * h_ref[...] + mix, g_ref[...], b_ref[...])


def _out_projection_ln(o2d, w_bf16, h2d, g, b, *, alpha, tm):
    m, d = h2d.shape
    kern = functools.partial(_outproj_ln_kernel, alpha=alpha)
    return pl.pallas_call(
        kern,
        grid=(m // tm,),
        in_specs=[
            pl.BlockSpec((tm, d), lambda i: (i, 0)),
            pl.BlockSpec((d, d), lambda i: (0, 0)),
            pl.BlockSpec((tm, d), lambda i: (i, 0)),
            pl.BlockSpec((1, d), lambda i: (0, 0)),
            pl.BlockSpec((1, d), lambda i: (0, 0)),
        ],
        out_specs=pl.BlockSpec((tm, d), lambda i: (i, 0)),
        out_shape=jax.ShapeDtypeStruct((m, d), F32),
        compiler_params=_cparams(("parallel",)),
        name="out_projection_ln",
    )(o2d, w_bf16, h2d, g, b)


def _router_kernel(h_ref, w_ref, b_ref, idx_ref, gate_ref, rank_ref, cnt_ref, carry_ref, *, n_experts):
    step = pl.program_id(0)

    @pl.when(step == 0)
    def _():
        carry_ref[...] = jnp.zeros(carry_ref.shape, F32)

    tm = h_ref.shape[0]
    logits = jnp.dot(h_ref[...], w_ref[...], preferred_element_type=F32,
                     precision=lax.Precision.HIGHEST) + b_ref[...]
    lane = lax.broadcasted_iota(jnp.int32, (tm, LANES), 1)
    work = jnp.where(lane < n_experts, logits, -jnp.inf)

    vals, idxs = [], []
    onehot = jnp.zeros((tm, LANES), F32)
    for _ in range(TOP_K):
        mx = jnp.max(work, axis=-1, keepdims=True)
        ix = jnp.min(jnp.where(work == mx, lane, LANES), axis=-1, keepdims=True)
        sel = lane == ix
        vals.append(mx)
        idxs.append(ix)
        onehot = jnp.where(sel, 1.0, onehot)
        work = jnp.where(sel, -jnp.inf, work)

    exps = [jnp.exp(v - vals[0]) for v in vals]
    denom = exps[0]
    for e in exps[1:]:
        denom = denom + e

    r_i = lax.broadcasted_iota(jnp.int32, (tm, tm), 0)
    c_i = lax.broadcasted_iota(jnp.int32, (tm, tm), 1)
    lower = (c_i < r_i).astype(BF16)
    earlier = jnp.dot(lower, onehot.astype(BF16), preferred_element_type=F32) + carry_ref[...]

    idx_out = jnp.zeros((tm, LANES), jnp.int32)
    gate_out = jnp.zeros((tm, LANES), F32)
    rank_out = jnp.zeros((tm, LANES), jnp.int32)
    for k in range(TOP_K):
        rk = jnp.sum(jnp.where(lane == idxs[k], earlier, 0.0), axis=-1, keepdims=True)
        idx_out = jnp.where(lane == k, idxs[k], idx_out)
        gate_out = jnp.where(lane == k, exps[k] / denom, gate_out)
        rank_out = jnp.where(lane == k, rk.astype(jnp.int32), rank_out)
    idx_ref[...] = idx_out
    gate_ref[...] = gate_out
    rank_ref[...] = rank_out

    total = carry_ref[...] + jnp.sum(onehot, axis=0, keepdims=True)
    carry_ref[...] = total
    cnt_ref[...] = total.astype(jnp.int32)


def _router(h2d, w_pad, b_pad, *, n_experts, tm):
    t, d = h2d.shape
    kern = functools.partial(_router_kernel, n_experts=n_experts)
    tok_spec = pl.BlockSpec((tm, LANES), lambda i: (i, 0))
    return pl.pallas_call(
        kern,
        grid=(t // tm,),
        in_specs=[
            pl.BlockSpec((tm, d), lambda i: (i, 0)),
            pl.BlockSpec((d, LANES), lambda i: (0, 0)),
            pl.BlockSpec((1, LANES), lambda i: (0, 0)),
        ],
        out_specs=[tok_spec, tok_spec, tok_spec, pl.BlockSpec((1, LANES), lambda i: (0, 0))],
        out_shape=[jax.ShapeDtypeStruct((t, LANES), jnp.int32), jax.ShapeDtypeStruct((t, LANES), F32),
                   jax.ShapeDtypeStruct((t, LANES), jnp.int32), jax.ShapeDtypeStruct((1, LANES), jnp.int32)],
        scratch_shapes=[pltpu.VMEM((1, LANES), F32)],
        compiler_params=_cparams(("arbitrary",)),
        name="router",
    )(h2d, w_pad, b_pad)


def _dispatch_kernel(dest_ref, pad_ref, h_ref, x_hbm, zero_ref, sem, *, tm, tail_start, tail_rows):
    step = pl.program_id(0)
    base = step * tm

    @pl.when(step == 0)
    def _():
        zero_ref[...] = jnp.zeros(zero_ref.shape, F32)

        def tail_copy(n):
            return pltpu.make_async_copy(zero_ref, x_hbm.at[pl.ds(tail_start + n * ZERO_ROWS, ZERO_ROWS), :], sem)

        for n in range(tail_rows // ZERO_ROWS):
            tail_copy(n).start()
        for n in range(tail_rows // ZERO_ROWS):
            tail_copy(n).wait()

        def pad_copy(i):
            return pltpu.make_async_copy(h_ref.at[pl.ds(0, 1), :], x_hbm.at[pl.ds(pad_ref[i], 1), :], sem)

        def pad_start(i, carry):
            @pl.when(pad_ref[i] >= 0)
            def _():
                pad_copy(i).start()
            return carry

        def pad_wait(i, carry):
            @pl.when(pad_ref[i] >= 0)
            def _():
                pad_copy(i).wait()
            return carry

        lax.fori_loop(0, pad_ref.shape[0], pad_start, 0)
        lax.fori_loop(0, pad_ref.shape[0], pad_wait, 0)

    def row_copy(t, k):
        r = dest_ref[(base + t) * TOP_K + k]
        return pltpu.make_async_copy(h_ref.at[pl.ds(t, 1), :], x_hbm.at[pl.ds(r, 1), :], sem)

    def issue(t, carry):
        for k in range(TOP_K):
            row_copy(t, k).start()
        return carry

    lax.fori_loop(0, tm, issue, 0, unroll=DMA_ISSUE_UNROLL)
    for _ in range(TOP_K):
        pltpu.make_async_copy(h_ref, x_hbm.at[pl.ds(0, tm), :], sem).wait()


def _dispatch(dest_flat, pad_rows, h2d, *, p_rows, tm):
    t, d = h2d.shape
    tail_start = t * TOP_K
    tail_rows = p_rows - tail_start
    assert tail_rows % ZERO_ROWS == 0
    kern = functools.partial(_dispatch_kernel, tm=tm, tail_start=tail_start, tail_rows=tail_rows)
    return pl.pallas_call(
        kern,
        grid_spec=pltpu.PrefetchScalarGridSpec(
            num_scalar_prefetch=2,
            grid=(t // tm,),
            in_specs=[pl.BlockSpec((tm, d), lambda i, dest, pad: (i, 0))],
            out_specs=pl.BlockSpec(memory_space=pl.ANY),
            scratch_shapes=[pltpu.VMEM((ZERO_ROWS, d), F32), pltpu.SemaphoreType.DMA(())],
        ),
        out_shape=jax.ShapeDtypeStruct((p_rows, d), F32),
        compiler_params=_cparams(("arbitrary",)),
        name="dispatch",
    )(dest_flat, pad_rows, h2d)


def _expert_kernel(ce_ref, cr_ref, cn_ref, x_hbm, w1g_ref, w1l_ref, b1g_ref, b1l_ref, w2_ref, b2_ref, y_hbm,
                   xb_ref, y_ref, stage_ref, w1g_b, w1l_b, w2_b, sem, *, nf, n_chunks):
    del ce_ref
    s = pl.program_id(0)
    item = jnp.maximum(s - 1, 0)
    c = item // nf
    f = item % nf
    nv = jnp.where(s > 0, cn_ref[c], 0)
    d = y_ref.shape[2]
    yslot = c % 2
    cur = (s + 1) % 2
    nxt = s % 2

    def cast_next_weights():
        w1g_b[nxt] = w1g_ref[...].astype(BF16)
        w1l_b[nxt] = w1l_ref[...].astype(BF16)
        w2_b[nxt] = w2_ref[...].astype(BF16)

    @pl.when(s == 0)
    def _():
        cast_next_weights()

    def x_copy(chunk):
        start = pl.multiple_of(cr_ref[chunk], SUBLANES)
        return pltpu.make_async_copy(x_hbm.at[pl.ds(start, CHUNK_ROWS), :], stage_ref, sem.at[0])

    @pl.when((f == 0) & (nv > 0) & (c == 0))
    def _():
        x_copy(c).start()

    @pl.when((f == 0) & (nv > 0))
    def _():
        x_copy(c).wait()
        xb_ref[...] = stage_ref[...].astype(BF16)
        y_ref[yslot] = jnp.broadcast_to(b2_ref[...], y_ref.shape[1:])

    c_next = jnp.minimum(c + 1, n_chunks - 1)

    @pl.when((f == 1) & (nv > 0) & (c + 1 < n_chunks) & (cn_ref[c_next] > 0))
    def _():
        x_copy(c_next).start()

    def compute(rows):
        cast_next_weights()
        blk = PIPE_ROWS if rows % PIPE_ROWS == 0 else rows
        col = min(512, d)

        def first(r0):
            xs = xb_ref[r0:r0 + blk, :]
            hg = jnp.dot(xs, w1g_b[cur], preferred_element_type=F32) + b1g_ref[...]
            hl = jnp.dot(xs, w1l_b[cur], preferred_element_type=F32) + b1l_ref[...]
            return hg, hl

        def activate(hg, hl):
            glu = jnp.minimum(hg, SWIGLU_LIMIT)
            lin = jnp.clip(hl, -SWIGLU_LIMIT, SWIGLU_LIMIT)
            return (glu * jax.nn.sigmoid(SWIGLU_ALPHA * glu) * (lin + 1.0)).astype(BF16)

        def second(r0, act):
            for n in range(d // col):
                y_ref[yslot, r0:r0 + blk, n * col:(n + 1) * col] += jnp.dot(
                    act, w2_b[cur, :, n * col:(n + 1) * col], preferred_element_type=F32)

        starts = list(range(0, rows, blk))
        h = first(starts[0])
        for k, r0 in enumerate(starts):
            act = activate(*h)
            if k + 1 < len(starts):
                h = first(starts[k + 1])
            second(r0, act)

    @pl.when(nv > SMALL_ROWS)
    def _():
        compute(CHUNK_ROWS)

    @pl.when((nv > 0) & (nv <= SMALL_ROWS))
    def _():
        compute(SMALL_ROWS)

    sizes = []
    size = SUBLANES
    while size <= CHUNK_ROWS:
        sizes.append(size)
        size *= 2

    def writeback(chunk, action):
        rows_valid = cn_ref[chunk]
        start = pl.multiple_of(cr_ref[chunk], SUBLANES)
        slot = chunk % 2
        for size in sizes:
            @pl.when((rows_valid & size) != 0)
            def _():
                off = pl.multiple_of(rows_valid - (rows_valid % (2 * size)), SUBLANES)
                cp = pltpu.make_async_copy(y_ref.at[slot, pl.ds(off, size), :],
                                           y_hbm.at[pl.ds(start + off, size), :], sem.at[1 + slot])
                if action == "start":
                    cp.start()
                else:
                    cp.wait()

    last_step_of_chunk = (f == nf - 1) & (nv > 0)

    @pl.when(last_step_of_chunk & (c > 0))
    def _():
        writeback(jnp.maximum(c - 1, 0), "wait")

    @pl.when(last_step_of_chunk)
    def _():
        writeback(c, "start")

    @pl.when(last_step_of_chunk & ((c + 1 >= n_chunks) | (cn_ref[c_next] == 0)))
    def _():
        writeback(c, "wait")


def _expert_ffn(chunk_e, chunk_row, chunk_nv, n_steps, x_buf, w1, b1, w2, b2, *, layer, tf):
    p, d = x_buf.shape
    n_experts, _, f2 = w1.shape[1:]
    d_ff = f2 // 2
    nf = d_ff // tf
    nc = chunk_e.shape[0]
    b1r = b1.reshape(b1.shape[0], n_experts, 1, f2)
    b2r = b2.reshape(b2.shape[0], n_experts, 1, d)

    n_items = nc * nf

    def expert_and_tile(item, ce, cn):
        c = item // nf
        return ce[c], jnp.where(cn[c] > 0, item % nf, nf - 1)

    def weights_of_next(s, ce, cn):
        return expert_and_tile(jnp.minimum(s, n_items - 1), ce, cn)

    def biases_of_current(s, ce, cn):
        return expert_and_tile(jnp.maximum(s - 1, 0), ce, cn)

    def w1_map(half):
        def index(s, ce, cr, cn):
            e, f = weights_of_next(s, ce, cn)
            return layer, e, 0, half * nf + f
        return index

    def b1_map(half):
        def index(s, ce, cr, cn):
            e, f = biases_of_current(s, ce, cn)
            return layer, e, 0, half * nf + f
        return index

    def w2_map(s, ce, cr, cn):
        e, f = weights_of_next(s, ce, cn)
        return layer, e, f, 0

    def b2_map(s, ce, cr, cn):
        e, _ = biases_of_current(s, ce, cn)
        return layer, e, 0, 0

    assert nf >= 2, "the next chunk's rows are requested during a chunk's second d_ff tile"
    kern = functools.partial(_expert_kernel, nf=nf, n_chunks=nc)
    return pl.pallas_call(
        kern,
        grid_spec=pltpu.PrefetchScalarGridSpec(
            num_scalar_prefetch=3,
            grid=(n_steps,),
            in_specs=[
                pl.BlockSpec(memory_space=pl.ANY),
                pl.BlockSpec((None, None, d, tf), w1_map(0)),
                pl.BlockSpec((None, None, d, tf), w1_map(1)),
                pl.BlockSpec((None, None, 1, tf), b1_map(0)),
                pl.BlockSpec((None, None, 1, tf), b1_map(1)),
                pl.BlockSpec((None, None, tf, d), w2_map),
                pl.BlockSpec((None, None, 1, d), b2_map),
            ],
            out_specs=pl.BlockSpec(memory_space=pl.ANY),
            scratch_shapes=[
                pltpu.VMEM((CHUNK_ROWS, d), BF16),
                pltpu.VMEM((2, CHUNK_ROWS, d), F32),
                pltpu.VMEM((CHUNK_ROWS, d), F32),
                pltpu.VMEM((2, d, tf), BF16),
                pltpu.VMEM((2, d, tf), BF16),
                pltpu.VMEM((2, tf, d), BF16),
                pltpu.SemaphoreType.DMA((3,)),
            ],
        ),
        out_shape=jax.ShapeDtypeStruct((p, d), F32),
        input_output_aliases={3: 0},
        compiler_params=_cparams(("arbitrary",)),
        name="expert_ffn",
    )(chunk_e, chunk_row, chunk_nv, x_buf, w1, w1, b1r, b1r, w2, b2r)


def _combine_kernel(dest_ref, y_hbm, gate_ref, h_ref, g_ref, b_ref, out_ref, buf_ref, sem, *, tm, alpha):
    step = pl.program_id(0)
    slot = step % 2

    def gather(tile, into):
        def issue(t, carry):
            for k in range(TOP_K):
                r = dest_ref[(tile * tm + t) * TOP_K + k]
                pltpu.make_async_copy(y_hbm.at[pl.ds(r, 1), :], buf_ref.at[into, k, pl.ds(t, 1), :],
                                      sem.at[into]).start()
            return carry
        lax.fori_loop(0, tm, issue, 0, unroll=DMA_ISSUE_UNROLL)

    @pl.when(step == 0)
    def _():
        gather(0, 0)

    @pl.when(step + 1 < pl.num_programs(0))
    def _():
        gather(step + 1, 1 - slot)

    for k in range(TOP_K):
        pltpu.make_async_copy(y_hbm.at[pl.ds(0, tm), :], buf_ref.at[slot, k], sem.at[slot]).wait()

    gates = gate_ref[...]
    y = gates[:, 0:1] * buf_ref[slot, 0]
    for k in range(1, TOP_K):
        y = y + gates[:, k:k + 1] * buf_ref[slot, k]
    out_ref[...] = _layer_norm_rows(alpha * h_ref[...] + y, g_ref[...], b_ref[...])


def _combine_ln(dest_flat, y_buf, gates, h2d, g, b, *, alpha, tm):
    t, d = h2d.shape
    kern = functools.partial(_combine_kernel, tm=tm, alpha=alpha)
    return pl.pallas_call(
        kern,
        grid_spec=pltpu.PrefetchScalarGridSpec(
            num_scalar_prefetch=1,
            grid=(t // tm,),
            in_specs=[
                pl.BlockSpec(memory_space=pl.ANY),
                pl.BlockSpec((tm, LANES), lambda i, dest: (i, 0)),
                pl.BlockSpec((tm, d), lambda i, dest: (i, 0)),
                pl.BlockSpec((1, d), lambda i, dest: (0, 0)),
                pl.BlockSpec((1, d), lambda i, dest: (0, 0)),
            ],
            out_specs=pl.BlockSpec((tm, d), lambda i, dest: (i, 0)),
            scratch_shapes=[pltpu.VMEM((2, TOP_K, tm, d), F32), pltpu.SemaphoreType.DMA((2,))],
        ),
        out_shape=jax.ShapeDtypeStruct((t, d), F32),
        compiler_params=_cparams(("arbitrary",)),
        name="combine_ln",
    )(dest_flat, y_buf, gates, h2d, g, b)


def _routing_tables(idx, rank, counts, *, n_chunks):
    padded = (counts + SUBLANES - 1) // SUBLANES * SUBLANES
    pad_end = jnp.cumsum(padded)
    pad_start = pad_end - padded
    experts = jnp.arange(counts.shape[0], dtype=idx.dtype)
    dest = jnp.sum(jnp.where(idx[..., None] == experts, pad_start, 0), axis=-1) + rank
    nchunk = (padded + CHUNK_ROWS - 1) // CHUNK_ROWS
    chunk_end = jnp.cumsum(nchunk)
    chunk_start = chunk_end - nchunk
    total = chunk_end[-1]
    cids = jnp.arange(n_chunks, dtype=jnp.int32)
    clamped = jnp.minimum(cids, total - 1)
    ce = jnp.sum((chunk_end[None, :] <= clamped[:, None]).astype(jnp.int32), axis=1)
    local = clamped - chunk_start[ce]
    crow = (pad_start[ce] + local * CHUNK_ROWS).astype(jnp.int32)
    cnv = jnp.clip(padded[ce] - local * CHUNK_ROWS, 0, CHUNK_ROWS)
    cnv = jnp.where(cids < total, cnv, 0).astype(jnp.int32)
    fill = counts[:, None] + jnp.arange(SUBLANES, dtype=jnp.int32)[None, :]
    pad_rows = jnp.where(fill < padded[:, None], pad_start[:, None] + fill, -1)
    return (dest.reshape(-1).astype(jnp.int32), pad_rows.reshape(-1).astype(jnp.int32),
            ce.astype(jnp.int32), crow, cnv)


def _moe_ln(h2d, router_w, router_b, w1, b1, w2, b2, g, b, *, layer, alpha, tiles):
    t, d = h2d.shape
    n_experts = router_w.shape[1]
    w_pad = jnp.pad(router_w, ((0, 0), (0, LANES - n_experts)))
    b_pad = jnp.pad(router_b, (0, LANES - n_experts)).reshape(1, LANES)
    idx, gates, rank, counts = _router(h2d, w_pad, b_pad, n_experts=n_experts, tm=tiles["router_tm"])
    n_chunks = (t * TOP_K + n_experts * SUBLANES) // CHUNK_ROWS + n_experts
    dest_flat, pad_rows, ce, crow, cnv = _routing_tables(idx[:, :TOP_K], rank[:, :TOP_K],
                                                         counts[0, :n_experts], n_chunks=n_chunks)
    p_rows = t * TOP_K + n_experts * SUBLANES + CHUNK_ROWS
    x_buf = _dispatch(dest_flat, pad_rows, h2d, p_rows=p_rows, tm=tiles["dispatch_tm"])
    used_chunks = jnp.sum((cnv > 0).astype(jnp.int32))
    n_steps = used_chunks * (w2.shape[2] // tiles["expert_tf"]) + 1
    y_buf = _expert_ffn(ce, crow, cnv, n_steps, x_buf, w1, b1, w2, b2, layer=layer, tf=tiles["expert_tf"])
    return _combine_ln(dest_flat, y_buf, gates, h2d, g, b, alpha=alpha, tm=tiles["combine_tm"])


def _rope_tables(positions, q_scale):
    inv_freq = ROPE_THETA ** (-jnp.arange(0, HEAD_DIM, 2, dtype=F32) / HEAD_DIM)
    ang = positions.astype(F32).reshape(-1)[:, None] * inv_freq
    cos, sin = jnp.cos(ang), jnp.sin(ang)
    cos2 = jnp.concatenate([cos, cos], axis=-1)
    sin2 = jnp.concatenate([-sin, sin], axis=-1)
    k_tab = jnp.stack([cos2, sin2])
    return jnp.stack([k_tab * q_scale, k_tab])


def _tiles(t, s, d, d_ff):
    return {
        "inproj_tm": min(1024, t), "inproj_tn": min(1024, d),
        "da_tq": min(512, s), "da_tk": min(512, s),
        "sb_tq": min(512, s), "sb_tk": min(256, s), "sb_heads_per_step": min(4, d // HEAD_DIM),
        "outproj_tm": min(512, t),
        "router_tm": min(256, t),
        "dispatch_tm": min(512, t),
        "combine_tm": min(256, t),
        "expert_tf": min(256, d_ff),
    }


def kernel(x, positions, da_w_in, da_lambda_q1, da_lambda_k1, da_lambda_q2, da_lambda_k2, da_subln_g, da_w_out,
           sb_w_in, sb_w_out, ln1_g, ln1_b, ln2_g, ln2_b, router_w, router_b, expert_w1, expert_b1, expert_w2,
           expert_b2):
    bsz, seq, d = x.shape
    t = bsz * seq
    depth = ln1_g.shape[0]
    d_ff = expert_w2.shape[2]
    alpha = (2 * depth) ** 0.25
    q_scale = HEAD_DIM ** -0.5 * math.log2(math.e)
    tiles = _tiles(t, seq, d, d_ff)
    tab = _rope_tables(positions, q_scale)

    h = x.reshape(t, d)
    for i in range(depth):
        j = i // 2
        if i % 2 == 0:
            lambda_init = 0.8 - 0.6 * math.exp(-0.3 * i)
            proj = _in_projection(h, da_w_in[j].astype(BF16), tab, rope=True, d_model=d, q_scale=q_scale,
                                  tm=tiles["inproj_tm"], tn=tiles["inproj_tn"])
            lam_rows = jnp.stack([da_lambda_q1[j], da_lambda_k1[j], da_lambda_q2[j], da_lambda_k2[j]])
            o = _diff_attention(proj.reshape(bsz, seq, 3 * d), lam_rows, da_subln_g[j].reshape(1, DA_V_DIM),
                                lambda_init=lambda_init, tq=tiles["da_tq"], tk=tiles["da_tk"])
            w_out = da_w_out[j]
        else:
            proj = _in_projection(h, sb_w_in[j].astype(BF16), tab, rope=False, d_model=d, q_scale=q_scale,
                                  tm=tiles["inproj_tm"], tn=tiles["inproj_tn"])
            o = _stick_breaking_attention(proj.reshape(bsz, seq, 3 * d), tq=tiles["sb_tq"], tk=tiles["sb_tk"],
                                          hp=tiles["sb_heads_per_step"])
            w_out = sb_w_out[j]
        h = _out_projection_ln(o.reshape(t, d), w_out.astype(BF16), h, ln1_g[i].reshape(1, d),
                               ln1_b[i].reshape(1, d), alpha=alpha, tm=tiles["outproj_tm"])
        h = _moe_ln(h, router_w[i], router_b[i], expert_w1, expert_b1, expert_w2, expert_b2,
                    ln2_g[i].reshape(1, d), ln2_b[i].reshape(1, d), layer=i, alpha=alpha, tiles=tiles)
    return h.reshape(bsz, seq, d)
```

```python
import functools
import math

import jax
import jax.numpy as jnp
from jax import lax
from jax.experimental import pallas as pl
from jax.experimental.pallas import tpu as pltpu

F32 = jnp.float32
BF16 = jnp.bfloat16

HEAD_DIM = 128
DA_V_DIM = 2 * HEAD_DIM
TOP_K = 4
SWIGLU_ALPHA = 1.702
SWIGLU_LIMIT = 7.0
ROPE_THETA = 10000.0
LN_EPS = 1e-5

LANES = 128
SUBLANES = 8
VMEM_LIMIT_BYTES = 56 * 1024 * 1024

CHUNK_ROWS = 1088
SMALL_ROWS = 128
PIPE_ROWS = CHUNK_ROWS
ZERO_ROWS = 64
DMA_ISSUE_UNROLL = 8


def _cparams(sem):
    return pltpu.CompilerParams(dimension_semantics=sem, vmem_limit_bytes=VMEM_LIMIT_BYTES)


def _layer_norm_rows(z, g, b):
    mu = jnp.mean(z, axis=-1, keepdims=True)
    zc = z - mu
    var = jnp.mean(zc * zc, axis=-1, keepdims=True)
    return zc * lax.rsqrt(var + LN_EPS) * g + b


def _inproj_kernel(x_ref, w_ref, tab_ref, o_ref, xb_ref, *, rope, qk_tiles, q_tiles, q_scale):
    j = pl.program_id(1)

    @pl.when(j == 0)
    def _():
        xb_ref[...] = x_ref[...].astype(BF16)

    acc = jnp.dot(xb_ref[...], w_ref[...], preferred_element_type=F32)
    tn = acc.shape[1]

    if rope:
        @pl.when(j < qk_tiles)
        def _():
            cos = tab_ref[0]
            sin = tab_ref[1]
            for c in range(tn // LANES):
                xc = acc[:, c * LANES:(c + 1) * LANES]
                rot = pltpu.roll(xc, LANES // 2, 1)
                o_ref[:, c * LANES:(c + 1) * LANES] = (xc * cos + rot * sin).astype(o_ref.dtype)

        @pl.when(j >= qk_tiles)
        def _():
            o_ref[...] = acc.astype(o_ref.dtype)
    else:
        @pl.when(j < q_tiles)
        def _():
            o_ref[...] = (acc * q_scale).astype(o_ref.dtype)

        @pl.when(j >= q_tiles)
        def _():
            o_ref[...] = acc.astype(o_ref.dtype)


def _in_projection(x2d, w_bf16, tab, *, rope, d_model, q_scale, tm, tn):
    m, k = x2d.shape
    n = w_bf16.shape[1]
    q_tiles = d_model // tn
    qk_tiles = 2 * q_tiles
    kern = functools.partial(_inproj_kernel, rope=rope, qk_tiles=qk_tiles, q_tiles=q_tiles, q_scale=q_scale)
    return pl.pallas_call(
        kern,
        grid=(m // tm, n // tn),
        in_specs=[
            pl.BlockSpec((tm, k), lambda i, j: (i, 0)),
            pl.BlockSpec((k, tn), lambda i, j: (0, j)),
            pl.BlockSpec((None, 2, tm, LANES), lambda i, j: (jnp.minimum(j // q_tiles, 1), 0, i, 0)),
        ],
        out_specs=pl.BlockSpec((tm, tn), lambda i, j: (i, j)),
        out_shape=jax.ShapeDtypeStruct((m, n), BF16),
        scratch_shapes=[pltpu.VMEM((tm, k), BF16)],
        compiler_params=_cparams(("parallel", "arbitrary")),
        name="in_projection",
    )(x2d, w_bf16, tab)


def _transpose_bf16(x):
    return x.astype(F32).T.astype(BF16)


def _store_v_transposed(v_ref, vt_ref, tk):
    for n in range(vt_ref.shape[0]):
        vt_ref[n] = _transpose_bf16(v_ref[n * tk:(n + 1) * tk, :])


def _da_kernel(q_ref, k_ref, v_ref, lam_ref, g_ref, o_ref, vt_ref, acc_ref, m_ref, l_ref, *, tq, tk, lambda_init):
    i = pl.program_id(2)
    diag_tiles = tq // tk

    @pl.when(i == 0)
    def _():
        _store_v_transposed(v_ref, vt_ref, tk)

    m_ref[...] = jnp.full(m_ref.shape, -jnp.inf, F32)
    l_ref[...] = jnp.zeros(l_ref.shape, F32)
    acc_ref[...] = jnp.zeros(acc_ref.shape, F32)
    q_t = _transpose_bf16(q_ref[...])
    key = lax.broadcasted_iota(jnp.int32, (tk, tq), 0)
    qry = lax.broadcasted_iota(jnp.int32, (tk, tq), 1)

    def step(j, shift):
        off = pl.multiple_of(j * tk, tk)
        ks = k_ref[pl.ds(off, tk), :]
        v_t = vt_ref[j]
        comps = range(2)
        s = [jnp.dot(ks[:, c * HEAD_DIM:(c + 1) * HEAD_DIM], q_t[c * HEAD_DIM:(c + 1) * HEAD_DIM, :],
                     preferred_element_type=F32) for c in comps]
        if shift is not None:
            visible = key + shift <= qry
            s = [jnp.where(visible, sc, -jnp.inf) for sc in s]
        m_old = [m_ref[c] for c in comps]
        m_new = [jnp.maximum(m_old[c], jnp.max(s[c], axis=0, keepdims=True)) for c in comps]
        alpha = [jnp.exp2(m_old[c] - m_new[c]) for c in comps]
        p = [jnp.exp2(s[c] - m_new[c]) for c in comps]
        for c in comps:
            l_ref[c] = alpha[c] * l_ref[c] + jnp.sum(p[c], axis=0, keepdims=True)
            m_ref[c] = m_new[c]
        pv = [jnp.dot(v_t, p[c].astype(BF16), preferred_element_type=F32) for c in comps]
        for c in comps:
            acc_ref[c] = alpha[c] * acc_ref[c] + pv[c]

    def body(j, carry):
        step(j, None)
        return carry

    lax.fori_loop(0, i * diag_tiles, body, 0)
    for dj in range(diag_tiles):
        step(i * diag_tiles + dj, dj * tk)

    lam_rows = lam_ref[...]
    lam = (jnp.exp(jnp.sum(lam_rows[0:1] * lam_rows[1:2], axis=-1, keepdims=True))
           - jnp.exp(jnp.sum(lam_rows[2:3] * lam_rows[3:4], axis=-1, keepdims=True)) + lambda_init)
    o_t = acc_ref[0] / l_ref[0] - lam * (acc_ref[1] / l_ref[1])
    o = o_t.T
    o = o * lax.rsqrt(jnp.mean(o * o, axis=-1, keepdims=True) + LN_EPS) * g_ref[...]
    o_ref[...] = (o * (1.0 - lambda_init)).astype(o_ref.dtype)


def _diff_attention(proj, lam_rows, subln_g, *, lambda_init, tq, tk):
    b, s, n3 = proj.shape
    d_model = n3 // 3
    heads = d_model // DA_V_DIM
    kern = functools.partial(_da_kernel, tq=tq, tk=tk, lambda_init=lambda_init)
    return pl.pallas_call(
        kern,
        grid=(b, heads, s // tq),
        in_specs=[
            pl.BlockSpec((None, tq, DA_V_DIM), lambda bi, h, i: (bi, i, h)),
            pl.BlockSpec((None, s, DA_V_DIM), lambda bi, h, i: (bi, 0, heads + h)),
            pl.BlockSpec((None, s, DA_V_DIM), lambda bi, h, i: (bi, 0, 2 * heads + h)),
            pl.BlockSpec((4, HEAD_DIM), lambda bi, h, i: (0, 0)),
            pl.BlockSpec((1, DA_V_DIM), lambda bi, h, i: (0, 0)),
        ],
        out_specs=pl.BlockSpec((None, tq, DA_V_DIM), lambda bi, h, i: (bi, i, h)),
        out_shape=jax.ShapeDtypeStruct((b, s, d_model), BF16),
        scratch_shapes=[pltpu.VMEM((s // tk, DA_V_DIM, tk), BF16), pltpu.VMEM((2, DA_V_DIM, tq), F32),
                        pltpu.VMEM((2, 1, tq), F32), pltpu.VMEM((2, 1, tq), F32)],
        compiler_params=_cparams(("parallel", "parallel", "arbitrary")),
        name="diff_attention",
    )(proj, proj, proj, lam_rows, subln_g)


def _sb_kernel(q_ref, k_ref, v_ref, o_ref, vt_ref, acc_ref, run_ref, *, tq, tk, hp):
    i = pl.program_id(2)
    diag_tiles = tq // tk
    heads = range(hp)

    def head(x, h, axis):
        return x[:, h * HEAD_DIM:(h + 1) * HEAD_DIM] if axis == 1 else x[h * HEAD_DIM:(h + 1) * HEAD_DIM, :]

    @pl.when(i == 0)
    def _():
        _store_v_transposed(v_ref, vt_ref, tk)

    acc_ref[...] = jnp.zeros(acc_ref.shape, F32)
    run_ref[...] = jnp.zeros(run_ref.shape, F32)
    q_t = _transpose_bf16(q_ref[...])
    key = lax.broadcasted_iota(jnp.int32, (tk, tq), 0)
    qry = lax.broadcasted_iota(jnp.int32, (tk, tq), 1)
    k_a = lax.broadcasted_iota(jnp.int32, (tk, tk), 0)
    k_b = lax.broadcasted_iota(jnp.int32, (tk, tk), 1)
    later = (k_b > k_a).astype(BF16)

    def step(j, shift):
        off = pl.multiple_of(j * tk, tk)
        ks = k_ref[pl.ds(off, tk), :]
        v_t = vt_ref[j]
        z = [jnp.dot(head(ks, h, 1), head(q_t, h, 0), preferred_element_type=F32) for h in heads]
        log_beta = [jnp.minimum(zh, 0.0) - jnp.log2(1.0 + jnp.exp2(-jnp.abs(zh))) for zh in z]
        log_1m = [log_beta[h] - z[h] for h in heads]
        if shift is not None:
            strict = key + shift < qry
            log_1m = [jnp.where(strict, x, 0.0) for x in log_1m]
        after = [jnp.dot(later, x.astype(BF16), preferred_element_type=F32) for x in log_1m]
        run = [run_ref[h] for h in heads]
        a = [jnp.exp2(log_beta[h] + after[h] + run[h]) for h in heads]
        if shift is not None:
            a = [jnp.where(strict, x, 0.0) for x in a]
        for h in heads:
            acc_ref[h] += jnp.dot(head(v_t, h, 0), a[h].astype(BF16), preferred_element_type=F32)
            run_ref[h] = run[h] + after[h][0:1, :] + log_1m[h][0:1, :]

    for dj in reversed(range(diag_tiles)):
        step(i * diag_tiles + dj, dj * tk)

    def body(jj, carry):
        step(i * diag_tiles - 1 - jj, None)
        return carry

    lax.fori_loop(0, i * diag_tiles, body, 0)
    for h in heads:
        o_ref[:, h * HEAD_DIM:(h + 1) * HEAD_DIM] = acc_ref[h].T.astype(o_ref.dtype)


def _stick_breaking_attention(proj, *, tq, tk, hp):
    b, s, n3 = proj.shape
    d_model = n3 // 3
    heads = d_model // HEAD_DIM
    groups = heads // hp
    width = hp * HEAD_DIM
    kern = functools.partial(_sb_kernel, tq=tq, tk=tk, hp=hp)
    return pl.pallas_call(
        kern,
        grid=(b, groups, s // tq),
        in_specs=[
            pl.BlockSpec((None, tq, width), lambda bi, h, i: (bi, i, h)),
            pl.BlockSpec((None, s, width), lambda bi, h, i: (bi, 0, groups + h)),
            pl.BlockSpec((None, s, width), lambda bi, h, i: (bi, 0, 2 * groups + h)),
        ],
        out_specs=pl.BlockSpec((None, tq, width), lambda bi, h, i: (bi, i, h)),
        out_shape=jax.ShapeDtypeStruct((b, s, d_model), BF16),
        scratch_shapes=[pltpu.VMEM((s // tk, width, tk), BF16), pltpu.VMEM((hp, HEAD_DIM, tq), F32),
                        pltpu.VMEM((hp, 1, tq), F32)],
        compiler_params=_cparams(("parallel", "parallel", "arbitrary")),
        name="stick_breaking_attention",
    )(proj, proj, proj)


def _outproj_ln_kernel(o_ref, w_ref, h_ref, g_ref, b_ref, out_ref, *, alpha):
    mix = jnp.dot(o_ref[...], w_ref[...], preferred_element_type=F32)
    out_ref[...] = _layer_norm_rows(alpha * h_ref[...] + mix, g_ref[...], b_ref[...])


def _out_projection_ln(o2d, w_bf16, h2d, g, b, *, alpha, tm):
    m, d = h2d.shape
    kern = functools.partial(_outproj_ln_kernel, alpha=alpha)
    return pl.pallas_call(
        kern,
        grid=(m // tm,),
        in_specs=[
            pl.BlockSpec((tm, d), lambda i: (i, 0)),
            pl.BlockSpec((d, d), lambda i: (0, 0)),
            pl.BlockSpec((tm, d), lambda i: (i, 0)),
            pl.BlockSpec((1, d), lambda i: (0, 0)),
            pl.BlockSpec((1, d), lambda i: (0, 0)),
        ],
        out_specs=pl.BlockSpec((tm, d), lambda i: (i, 0)),
        out_shape=jax.ShapeDtypeStruct((m, d), F32),
        compiler_params=_cparams(("parallel",)),
        name="out_projection_ln",
    )(o2d, w_bf16, h2d, g, b)


def _router_kernel(h_ref, w_ref, b_ref, idx_ref, gate_ref, rank_ref, cnt_ref, carry_ref, *, n_experts):
    step = pl.program_id(0)

    @pl.when(step == 0)
    def _():
        carry_ref[...] = jnp.zeros(carry_ref.shape, F32)

    tm = h_ref.shape[0]
    h = h_ref[...]
    h_hi = h.astype(BF16)
    h_lo = (h - h_hi.astype(F32)).astype(BF16)
    w_hi = w_ref[0]
    logits = (jnp.dot(h_hi, w_hi, preferred_element_type=F32)
              + jnp.dot(h_hi, w_ref[1], preferred_element_type=F32)
              + jnp.dot(h_lo, w_hi, preferred_element_type=F32)) + b_ref[...]
    lane = lax.broadcasted_iota(jnp.int32, (tm, LANES), 1)
    work = jnp.where(lane < n_experts, logits, -jnp.inf)

    vals, idxs = [], []
    onehot = jnp.zeros((tm, LANES), F32)
    for _ in range(TOP_K):
        mx = jnp.max(work, axis=-1, keepdims=True)
        ix = jnp.min(jnp.where(work == mx, lane, LANES), axis=-1, keepdims=True)
        sel = lane == ix
        vals.append(mx)
        idxs.append(ix)
        onehot = jnp.where(sel, 1.0, onehot)
        work = jnp.where(sel, -jnp.inf, work)

    exps = [jnp.exp(v - vals[0]) for v in vals]
    denom = exps[0]
    for e in exps[1:]:
        denom = denom + e

    r_i = lax.broadcasted_iota(jnp.int32, (tm, tm), 0)
    c_i = lax.broadcasted_iota(jnp.int32, (tm, tm), 1)
    lower = (c_i < r_i).astype(BF16)
    earlier = jnp.dot(lower, onehot.astype(BF16), preferred_element_type=F32) + carry_ref[...]

    idx_out = jnp.zeros((tm, LANES), jnp.int32)
    gate_out = jnp.zeros((tm, LANES), F32)
    rank_out = jnp.zeros((tm, LANES), jnp.int32)
    for k in range(TOP_K):
        rk = jnp.sum(jnp.where(lane == idxs[k], earlier, 0.0), axis=-1, keepdims=True)
        idx_out = jnp.where(lane == k, idxs[k], idx_out)
        gate_out = jnp.where(lane == k, exps[k] / denom, gate_out)
        rank_out = jnp.where(lane == k, rk.astype(jnp.int32), rank_out)
    idx_ref[...] = idx_out
    gate_ref[...] = gate_out
    rank_ref[...] = rank_out

    total = carry_ref[...] + jnp.sum(onehot, axis=0, keepdims=True)
    carry_ref[...] = total
    cnt_ref[...] = total.astype(jnp.int32)


def _router(h2d, w_pad, b_pad, *, n_experts, tm):
    t, d = h2d.shape
    kern = functools.partial(_router_kernel, n_experts=n_experts)
    tok_spec = pl.BlockSpec((tm, LANES), lambda i: (i, 0))
    return pl.pallas_call(
        kern,
        grid=(t // tm,),
        in_specs=[
            pl.BlockSpec((tm, d), lambda i: (i, 0)),
            pl.BlockSpec((2, d, LANES), lambda i: (0, 0, 0)),
            pl.BlockSpec((1, LANES), lambda i: (0, 0)),
        ],
        out_specs=[tok_spec, tok_spec, tok_spec, pl.BlockSpec((1, LANES), lambda i: (0, 0))],
        out_shape=[jax.ShapeDtypeStruct((t, LANES), jnp.int32), jax.ShapeDtypeStruct((t, LANES), F32),
                   jax.ShapeDtypeStruct((t, LANES), jnp.int32), jax.ShapeDtypeStruct((1, LANES), jnp.int32)],
        scratch_shapes=[pltpu.VMEM((1, LANES), F32)],
        compiler_params=_cparams(("arbitrary",)),
        name="router",
    )(h2d, w_pad, b_pad)


def _dispatch_kernel(dest_ref, pad_ref, h_ref, x_hbm, zero_ref, sem, *, tm, tail_start, tail_rows):
    step = pl.program_id(0)
    base = step * tm

    @pl.when(step == 0)
    def _():
        zero_ref[...] = jnp.zeros(zero_ref.shape, F32)

        def tail_copy(n):
            return pltpu.make_async_copy(zero_ref, x_hbm.at[pl.ds(tail_start + n * ZERO_ROWS, ZERO_ROWS), :], sem)

        for n in range(tail_rows // ZERO_ROWS):
            tail_copy(n).start()
        for n in range(tail_rows // ZERO_ROWS):
            tail_copy(n).wait()

        def pad_copy(i):
            return pltpu.make_async_copy(h_ref.at[pl.ds(0, 1), :], x_hbm.at[pl.ds(pad_ref[i], 1), :], sem)

        def pad_start(i, carry):
            @pl.when(pad_ref[i] >= 0)
            def _():
                pad_copy(i).start()
            return carry

        def pad_wait(i, carry):
            @pl.when(pad_ref[i] >= 0)
            def _():
                pad_copy(i).wait()
            return carry

        lax.fori_loop(0, pad_ref.shape[0], pad_start, 0)
        lax.fori_loop(0, pad_ref.shape[0], pad_wait, 0)

    def row_copy(t, k):
        r = dest_ref[(base + t) * TOP_K + k]
        return pltpu.make_async_copy(h_ref.at[pl.ds(t, 1), :], x_hbm.at[pl.ds(r, 1), :], sem)

    def issue(t, carry):
        for k in range(TOP_K):
            row_copy(t, k).start()
        return carry

    lax.fori_loop(0, tm, issue, 0, unroll=DMA_ISSUE_UNROLL)
    for _ in range(TOP_K):
        pltpu.make_async_copy(h_ref, x_hbm.at[pl.ds(0, tm), :], sem).wait()


def _dispatch(dest_flat, pad_rows, h2d, *, p_rows, tm):
    t, d = h2d.shape
    tail_start = t * TOP_K
    tail_rows = p_rows - tail_start
    assert tail_rows % ZERO_ROWS == 0
    kern = functools.partial(_dispatch_kernel, tm=tm, tail_start=tail_start, tail_rows=tail_rows)
    return pl.pallas_call(
        kern,
        grid_spec=pltpu.PrefetchScalarGridSpec(
            num_scalar_prefetch=2,
            grid=(t // tm,),
            in_specs=[pl.BlockSpec((tm, d), lambda i, dest, pad: (i, 0))],
            out_specs=pl.BlockSpec(memory_space=pl.ANY),
            scratch_shapes=[pltpu.VMEM((ZERO_ROWS, d), F32), pltpu.SemaphoreType.DMA(())],
        ),
        out_shape=jax.ShapeDtypeStruct((p_rows, d), F32),
        compiler_params=_cparams(("arbitrary",)),
        name="dispatch",
    )(dest_flat, pad_rows, h2d)


def _expert_kernel(ce_ref, cr_ref, cn_ref, x_hbm, w1g_ref, w1l_ref, b1g_ref, b1l_ref, w2_ref, b2_ref, y_hbm,
                   xb_ref, y_ref, stage_ref, w1g_b, w1l_b, w2_b, sem, *, nf, n_chunks):
    del ce_ref
    s = pl.program_id(0)
    item = jnp.maximum(s - 1, 0)
    c = item // nf
    f = item % nf
    nv = jnp.where(s > 0, cn_ref[c], 0)
    d = y_ref.shape[2]
    yslot = c % 2
    cur = (s + 1) % 2
    nxt = s % 2

    def cast_next_weights():
        w1g_b[nxt] = w1g_ref[...].astype(BF16)
        w1l_b[nxt] = w1l_ref[...].astype(BF16)
        w2_b[nxt] = w2_ref[...].astype(BF16)

    @pl.when(s == 0)
    def _():
        cast_next_weights()

    def x_copy(chunk):
        start = pl.multiple_of(cr_ref[chunk], SUBLANES)
        return pltpu.make_async_copy(x_hbm.at[pl.ds(start, CHUNK_ROWS), :], stage_ref, sem.at[0])

    @pl.when((f == 0) & (nv > 0) & (c == 0))
    def _():
        x_copy(c).start()

    @pl.when((f == 0) & (nv > 0))
    def _():
        x_copy(c).wait()
        xb_ref[...] = stage_ref[...].astype(BF16)
        y_ref[yslot] = jnp.broadcast_to(b2_ref[...], y_ref.shape[1:])

    c_next = jnp.minimum(c + 1, n_chunks - 1)

    @pl.when((f == 1) & (nv > 0) & (c + 1 < n_chunks) & (cn_ref[c_next] > 0))
    def _():
        x_copy(c_next).start()

    def compute(rows):
        cast_next_weights()
        blk = PIPE_ROWS if rows % PIPE_ROWS == 0 else rows
        col = min(512, d)

        def first(r0):
            xs = xb_ref[r0:r0 + blk, :]
            hg = jnp.dot(xs, w1g_b[cur], preferred_element_type=F32) + b1g_ref[...]
            hl = jnp.dot(xs, w1l_b[cur], preferred_element_type=F32) + b1l_ref[...]
            return hg, hl

        def activate(hg, hl):
            glu = jnp.minimum(hg, SWIGLU_LIMIT)
            lin = jnp.clip(hl, -SWIGLU_LIMIT, SWIGLU_LIMIT)
            return (glu * jax.nn.sigmoid(SWIGLU_ALPHA * glu) * (lin + 1.0)).astype(BF16)

        def second(r0, act):
            for n in range(d // col):
                y_ref[yslot, r0:r0 + blk, n * col:(n + 1) * col] += jnp.dot(
                    act, w2_b[cur, :, n * col:(n + 1) * col], preferred_element_type=F32)

        starts = list(range(0, rows, blk))
        h = first(starts[0])
        for k, r0 in enumerate(starts):
            act = activate(*h)
            if k + 1 < len(starts):
                h = first(starts[k + 1])
            second(r0, act)

    @pl.when(nv > SMALL_ROWS)
    def _():
        compute(CHUNK_ROWS)

    @pl.when((nv > 0) & (nv <= SMALL_ROWS))
    def _():
        compute(SMALL_ROWS)

    sizes = []
    size = SUBLANES
    while size <= CHUNK_ROWS:
        sizes.append(size)
        size *= 2

    def writeback(chunk, action):
        rows_valid = cn_ref[chunk]
        start = pl.multiple_of(cr_ref[chunk], SUBLANES)
        slot = chunk % 2
        for size in sizes:
            @pl.when((rows_valid & size) != 0)
            def _():
                off = pl.multiple_of(rows_valid - (rows_valid % (2 * size)), SUBLANES)
                cp = pltpu.make_async_copy(y_ref.at[slot, pl.ds(off, size), :],
                                           y_hbm.at[pl.ds(start + off, size), :], sem.at[1 + slot])
                if action == "start":
                    cp.start()
                else:
                    cp.wait()

    last_step_of_chunk = (f == nf - 1) & (nv > 0)

    @pl.when(last_step_of_chunk & (c > 0))
    def _():
        writeback(jnp.maximum(c - 1, 0), "wait")

    @pl.when(last_step_of_chunk)
    def _():
        writeback(c, "start")

    @pl.when(last_step_of_chunk & ((c + 1 >= n_chunks) | (cn_ref[c_next] == 0)))
    def _():
        writeback(c, "wait")


def _expert_ffn(chunk_e, chunk_row, chunk_nv, n_steps, x_buf, w1, b1, w2, b2, *, layer, tf):
    p, d = x_buf.shape
    n_experts, _, f2 = w1.shape[1:]
    d_ff = f2 // 2
    nf = d_ff // tf
    nc = chunk_e.shape[0]
    b1r = b1.reshape(b1.shape[0], n_experts, 1, f2)
    b2r = b2.reshape(b2.shape[0], n_experts, 1, d)

    n_items = nc * nf

    def expert_and_tile(item, ce, cn):
        c = item // nf
        return ce[c], jnp.where(cn[c] > 0, item % nf, nf - 1)

    def weights_of_next(s, ce, cn):
        return expert_and_tile(jnp.minimum(s, n_items - 1), ce, cn)

    def biases_of_current(s, ce, cn):
        return expert_and_tile(jnp.maximum(s - 1, 0), ce, cn)

    def w1_map(half):
        def index(s, ce, cr, cn):
            e, f = weights_of_next(s, ce, cn)
            return layer, e, 0, half * nf + f
        return index

    def b1_map(half):
        def index(s, ce, cr, cn):
            e, f = biases_of_current(s, ce, cn)
            return layer, e, 0, half * nf + f
        return index

    def w2_map(s, ce, cr, cn):
        e, f = weights_of_next(s, ce, cn)
        return layer, e, f, 0

    def b2_map(s, ce, cr, cn):
        e, _ = biases_of_current(s, ce, cn)
        return layer, e, 0, 0

    assert nf >= 2, "the next chunk's rows are requested during a chunk's second d_ff tile"
    kern = functools.partial(_expert_kernel, nf=nf, n_chunks=nc)
    return pl.pallas_call(
        kern,
        grid_spec=pltpu.PrefetchScalarGridSpec(
            num_scalar_prefetch=3,
            grid=(n_steps,),
            in_specs=[
                pl.BlockSpec(memory_space=pl.ANY),
                pl.BlockSpec((None, None, d, tf), w1_map(0)),
                pl.BlockSpec((None, None, d, tf), w1_map(1)),
                pl.BlockSpec((None, None, 1, tf), b1_map(0)),
                pl.BlockSpec((None, None, 1, tf), b1_map(1)),
                pl.BlockSpec((None, None, tf, d), w2_map),
                pl.BlockSpec((None, None, 1, d), b2_map),
            ],
            out_specs=pl.BlockSpec(memory_space=pl.ANY),
            scratch_shapes=[
                pltpu.VMEM((CHUNK_ROWS, d), BF16),
                pltpu.VMEM((2, CHUNK_ROWS, d), F32),
                pltpu.VMEM((CHUNK_ROWS, d), F32),
                pltpu.VMEM((2, d, tf), BF16),
                pltpu.VMEM((2, d, tf), BF16),
                pltpu.VMEM((2, tf, d), BF16),
                pltpu.SemaphoreType.DMA((3,)),
            ],
        ),
        out_shape=jax.ShapeDtypeStruct((p, d), F32),
        input_output_aliases={3: 0},
        compiler_params=_cparams(("arbitrary",)),
        name="expert_ffn",
    )(chunk_e, chunk_row, chunk_nv, x_buf, w1, w1, b1r, b1r, w2, b2r)


def _combine_kernel(dest_ref, y_hbm, gate_ref, h_ref, g_ref, b_ref, out_ref, buf_ref, sem, *, tm, alpha):
    step = pl.program_id(0)
    slot = step % 2

    def gather(tile, into):
        def issue(t, carry):
            for k in range(TOP_K):
                r = dest_ref[(tile * tm + t) * TOP_K + k]
                pltpu.make_async_copy(y_hbm.at[pl.ds(r, 1), :], buf_ref.at[into, k, pl.ds(t, 1), :],
                                      sem.at[into]).start()
            return carry
        lax.fori_loop(0, tm, issue, 0, unroll=DMA_ISSUE_UNROLL)

    @pl.when(step == 0)
    def _():
        gather(0, 0)

    @pl.when(step + 1 < pl.num_programs(0))
    def _():
        gather(step + 1, 1 - slot)

    for k in range(TOP_K):
        pltpu.make_async_copy(y_hbm.at[pl.ds(0, tm), :], buf_ref.at[slot, k], sem.at[slot]).wait()

    gates = gate_ref[...]
    y = gates[:, 0:1] * buf_ref[slot, 0]
    for k in range(1, TOP_K):
        y = y + gates[:, k:k + 1] * buf_ref[slot, k]
    out_ref[...] = _layer_norm_rows(alpha * h_ref[...] + y, g_ref[...], b_ref[...])


def _combine_ln(dest_flat, y_buf, gates, h2d, g, b, *, alpha, tm):
    t, d = h2d.shape
    kern = functools.partial(_combine_kernel, tm=tm, alpha=alpha)
    return pl.pallas_call(
        kern,
        grid_spec=pltpu.PrefetchScalarGridSpec(
            num_scalar_prefetch=1,
            grid=(t // tm,),
            in_specs=[
                pl.BlockSpec(memory_space=pl.ANY),
                pl.BlockSpec((tm, LANES), lambda i, dest: (i, 0)),
                pl.BlockSpec((tm, d), lambda i, dest: (i, 0)),
                pl.BlockSpec((1, d), lambda i, dest: (0, 0)),
                pl.BlockSpec((1, d), lambda i, dest: (0, 0)),
            ],
            out_specs=pl.BlockSpec((tm, d), lambda i, dest: (i, 0)),
            scratch_shapes=[pltpu.VMEM((2, TOP_K, tm, d), F32), pltpu.SemaphoreType.DMA((2,))],
        ),
        out_shape=jax.ShapeDtypeStruct((t, d), F32),
        compiler_params=_cparams(("arbitrary",)),
        name="combine_ln",
    )(dest_flat, y_buf, gates, h2d, g, b)


def _routing_tables(idx, rank, counts, *, n_chunks):
    padded = (counts + SUBLANES - 1) // SUBLANES * SUBLANES
    pad_end = jnp.cumsum(padded)
    pad_start = pad_end - padded
    experts = jnp.arange(counts.shape[0], dtype=idx.dtype)
    dest = jnp.sum(jnp.where(idx[..., None] == experts, pad_start, 0), axis=-1) + rank
    nchunk = (padded + CHUNK_ROWS - 1) // CHUNK_ROWS
    chunk_end = jnp.cumsum(nchunk)
    chunk_start = chunk_end - nchunk
    total = chunk_end[-1]
    cids = jnp.arange(n_chunks, dtype=jnp.int32)
    clamped = jnp.minimum(cids, total - 1)
    ce = jnp.sum((chunk_end[None, :] <= clamped[:, None]).astype(jnp.int32), axis=1)
    local = clamped - chunk_start[ce]
    crow = (pad_start[ce] + local * CHUNK_ROWS).astype(jnp.int32)
    cnv = jnp.clip(padded[ce] - local * CHUNK_ROWS, 0, CHUNK_ROWS)
    cnv = jnp.where(cids < total, cnv, 0).astype(jnp.int32)
    fill = counts[:, None] + jnp.arange(SUBLANES, dtype=jnp.int32)[None, :]
    pad_rows = jnp.where(fill < padded[:, None], pad_start[:, None] + fill, -1)
    return (dest.reshape(-1).astype(jnp.int32), pad_rows.reshape(-1).astype(jnp.int32),
            ce.astype(jnp.int32), crow, cnv)


def _moe_ln(h2d, router_w, router_b, w1, b1, w2, b2, g, b, *, layer, alpha, tiles):
    t, d = h2d.shape
    n_experts = router_w.shape[1]
    w_pad = jnp.pad(router_w, ((0, 0), (0, LANES - n_experts)))
    b_pad = jnp.pad(router_b, (0, LANES - n_experts)).reshape(1, LANES)
    w_hi = w_pad.astype(BF16)
    w_parts = jnp.stack([w_hi, (w_pad - w_hi.astype(F32)).astype(BF16)])
    idx, gates, rank, counts = _router(h2d, w_parts, b_pad, n_experts=n_experts, tm=tiles["router_tm"])
    n_chunks = (t * TOP_K + n_experts * SUBLANES) // CHUNK_ROWS + n_experts
    dest_flat, pad_rows, ce, crow, cnv = _routing_tables(idx[:, :TOP_K], rank[:, :TOP_K],
                                                         counts[0, :n_experts], n_chunks=n_chunks)
    p_rows = t * TOP_K + n_experts * SUBLANES + CHUNK_ROWS
    x_buf = _dispatch(dest_flat, pad_rows, h2d, p_rows=p_rows, tm=tiles["dispatch_tm"])
    used_chunks = jnp.sum((cnv > 0).astype(jnp.int32))
    n_steps = used_chunks * (w2.shape[2] // tiles["expert_tf"]) + 1
    y_buf = _expert_ffn(ce, crow, cnv, n_steps, x_buf, w1, b1, w2, b2, layer=layer, tf=tiles["expert_tf"])
    return _combine_ln(dest_flat, y_buf, gates, h2d, g, b, alpha=alpha, tm=tiles["combine_tm"])


def _rope_tables(positions, q_scale):
    inv_freq = ROPE_THETA ** (-jnp.arange(0, HEAD_DIM, 2, dtype=F32) / HEAD_DIM)
    ang = positions.astype(F32).reshape(-1)[:, None] * inv_freq
    cos, sin = jnp.cos(ang), jnp.sin(ang)
    cos2 = jnp.concatenate([cos, cos], axis=-1)
    sin2 = jnp.concatenate([-sin, sin], axis=-1)
    k_tab = jnp.stack([cos2, sin2])
    return jnp.stack([k_tab * q_scale, k_tab])


def _tiles(t, s, d, d_ff):
    return {
        "inproj_tm": min(1024, t), "inproj_tn": min(1024, d),
        "da_tq": min(512, s), "da_tk": min(512, s),
        "sb_tq": min(512, s), "sb_tk": min(256, s), "sb_heads_per_step": min(4, d // HEAD_DIM),
        "outproj_tm": min(512, t),
        "router_tm": min(256, t),
        "dispatch_tm": min(512, t),
        "combine_tm": min(256, t),
        "expert_tf": min(256, d_ff),
    }


def kernel(x, positions, da_w_in, da_lambda_q1, da_lambda_k1, da_lambda_q2, da_lambda_k2, da_subln_g, da_w_out,
           sb_w_in, sb_w_out, ln1_g, ln1_b, ln2_g, ln2_b, router_w, router_b, expert_w1, expert_b1, expert_w2,
           expert_b2):
    bsz, seq, d = x.shape
    t = bsz * seq
    depth = ln1_g.shape[0]
    d_ff = expert_w2.shape[2]
    alpha = (2 * depth) ** 0.25
    q_scale = HEAD_DIM ** -0.5 * math.log2(math.e)
    tiles = _tiles(t, seq, d, d_ff)
    tab = _rope_tables(positions, q_scale)

    h = x.reshape(t, d)
    for i in range(depth):
        j = i // 2
        if i % 2 == 0:
            lambda_init = 0.8 - 0.6 * math.exp(-0.3 * i)
            proj = _in_projection(h, da_w_in[j].astype(BF16), tab, rope=True, d_model=d, q_scale=q_scale,
                                  tm=tiles["inproj_tm"], tn=tiles["inproj_tn"])
            lam_rows = jnp.stack([da_lambda_q1[j], da_lambda_k1[j], da_lambda_q2[j], da_lambda_k2[j]])
            o = _diff_attention(proj.reshape(bsz, seq, 3 * d), lam_rows, da_subln_g[j].reshape(1, DA_V_DIM),
                                lambda_init=lambda_init, tq=tiles["da_tq"], tk=tiles["da_tk"])
            w_out = da_w_out[j]
        else:
            proj = _in_projection(h, sb_w_in[j].astype(BF16), tab, rope=False, d_model=d, q_scale=q_scale,
                                  tm=tiles["inproj_tm"], tn=tiles["inproj_tn"])
            o = _stick_breaking_attention(proj.reshape(bsz, seq, 3 * d), tq=tiles["sb_tq"], tk=tiles["sb_tk"],
                                          hp=tiles["sb_heads_per_step"])
            w_out = sb_w_out[j]
        h = _out_projection_ln(o.reshape(t, d), w_out.astype(BF16), h, ln1_g[i].reshape(1, d),
                               ln1_b[i].reshape(1, d), alpha=alpha, tm=tiles["outproj_tm"])
        h = _moe_ln(h, router_w[i], router_b[i], expert_w1, expert_b1, expert_w2, expert_b2,
                    ln2_g[i].reshape(1, d), ln2_b[i].reshape(1, d), layer=i, alpha=alpha, tiles=tiles)
    return h.reshape(bsz, seq, d)
```

```python
import functools
import math

import jax
import jax.numpy as jnp
from jax import lax
from jax.experimental import pallas as pl
from jax.experimental.pallas import tpu as pltpu

F32 = jnp.float32
BF16 = jnp.bfloat16

HEAD_DIM = 128
DA_V_DIM = 2 * HEAD_DIM
TOP_K = 4
SWIGLU_ALPHA = 1.702
SWIGLU_LIMIT = 7.0
ROPE_THETA = 10000.0
LN_EPS = 1e-5

LANES = 128
SUBLANES = 8
VMEM_LIMIT_BYTES = 56 * 1024 * 1024

CHUNK_ROWS = 1088
SMALL_ROWS = 128
PIPE_ROWS = CHUNK_ROWS
ZERO_ROWS = 64
DMA_ISSUE_UNROLL = 8


def _cparams(sem):
    return pltpu.CompilerParams(dimension_semantics=sem, vmem_limit_bytes=VMEM_LIMIT_BYTES)


def _layer_norm_rows(z, g, b):
    mu = jnp.mean(z, axis=-1, keepdims=True)
    zc = z - mu
    var = jnp.mean(zc * zc, axis=-1, keepdims=True)
    return zc * lax.rsqrt(var + LN_EPS) * g + b


def _inproj_kernel(x_ref, w_ref, tab_ref, o_ref, xb_ref, *, rope, qk_tiles, q_tiles, q_scale):
    j = pl.program_id(1)

    @pl.when(j == 0)
    def _():
        xb_ref[...] = x_ref[...].astype(BF16)

    acc = jnp.dot(xb_ref[...], w_ref[...], preferred_element_type=F32)
    tn = acc.shape[1]

    if rope:
        @pl.when(j < qk_tiles)
        def _():
            cos = tab_ref[0]
            sin = tab_ref[1]
            for c in range(tn // LANES):
                xc = acc[:, c * LANES:(c + 1) * LANES]
                rot = pltpu.roll(xc, LANES // 2, 1)
                o_ref[:, c * LANES:(c + 1) * LANES] = (xc * cos + rot * sin).astype(o_ref.dtype)

        @pl.when(j >= qk_tiles)
        def _():
            o_ref[...] = acc.astype(o_ref.dtype)
    else:
        @pl.when(j < q_tiles)
        def _():
            o_ref[...] = (acc * q_scale).astype(o_ref.dtype)

        @pl.when(j >= q_tiles)
        def _():
            o_ref[...] = acc.astype(o_ref.dtype)


def _in_projection(x2d, w_bf16, tab, *, rope, d_model, q_scale, tm, tn):
    m, k = x2d.shape
    n = w_bf16.shape[1]
    q_tiles = d_model // tn
    qk_tiles = 2 * q_tiles
    kern = functools.partial(_inproj_kernel, rope=rope, qk_tiles=qk_tiles, q_tiles=q_tiles, q_scale=q_scale)
    return pl.pallas_call(
        kern,
        grid=(m // tm, n // tn),
        in_specs=[
            pl.BlockSpec((tm, k), lambda i, j: (i, 0)),
            pl.BlockSpec((k, tn), lambda i, j: (0, j)),
            pl.BlockSpec((None, 2, tm, LANES), lambda i, j: (jnp.minimum(j // q_tiles, 1), 0, i, 0)),
        ],
        out_specs=pl.BlockSpec((tm, tn), lambda i, j: (i, j)),
        out_shape=jax.ShapeDtypeStruct((m, n), BF16),
        scratch_shapes=[pltpu.VMEM((tm, k), BF16)],
        compiler_params=_cparams(("parallel", "arbitrary")),
        name="in_projection",
    )(x2d, w_bf16, tab)


def _transpose_bf16(x):
    return x.astype(F32).T.astype(BF16)


def _store_v_transposed(v_ref, vt_ref, tk):
    for n in range(vt_ref.shape[0]):
        vt_ref[n] = _transpose_bf16(v_ref[n * tk:(n + 1) * tk, :])


def _da_kernel(q_ref, k_ref, v_ref, lam_ref, g_ref, o_ref, vt_ref, acc_ref, m_ref, l_ref, *, tq, tk, lambda_init):
    i = pl.program_id(2)
    diag_tiles = tq // tk

    @pl.when(i == 0)
    def _():
        _store_v_transposed(v_ref, vt_ref, tk)

    m_ref[...] = jnp.full(m_ref.shape, -jnp.inf, F32)
    l_ref[...] = jnp.zeros(l_ref.shape, F32)
    acc_ref[...] = jnp.zeros(acc_ref.shape, F32)
    q_t = _transpose_bf16(q_ref[...])
    key = lax.broadcasted_iota(jnp.int32, (tk, tq), 0)
    qry = lax.broadcasted_iota(jnp.int32, (tk, tq), 1)

    def step(j, shift):
        off = pl.multiple_of(j * tk, tk)
        ks = k_ref[pl.ds(off, tk), :]
        v_t = vt_ref[j]
        comps = range(2)
        s = [jnp.dot(ks[:, c * HEAD_DIM:(c + 1) * HEAD_DIM], q_t[c * HEAD_DIM:(c + 1) * HEAD_DIM, :],
                     preferred_element_type=F32) for c in comps]
        if shift is not None:
            visible = key + shift <= qry
            s = [jnp.where(visible, sc, -jnp.inf) for sc in s]
        m_old = [m_ref[c] for c in comps]
        m_new = [jnp.maximum(m_old[c], jnp.max(s[c], axis=0, keepdims=True)) for c in comps]
        alpha = [jnp.exp2(m_old[c] - m_new[c]) for c in comps]
        p = [jnp.exp2(s[c] - m_new[c]) for c in comps]
        for c in comps:
            l_ref[c] = alpha[c] * l_ref[c] + jnp.sum(p[c], axis=0, keepdims=True)
            m_ref[c] = m_new[c]
        pv = [jnp.dot(v_t, p[c].astype(BF16), preferred_element_type=F32) for c in comps]
        for c in comps:
            acc_ref[c] = alpha[c] * acc_ref[c] + pv[c]

    def body(j, carry):
        step(j, None)
        return carry

    lax.fori_loop(0, i * diag_tiles, body, 0)
    for dj in range(diag_tiles):
        step(i * diag_tiles + dj, dj * tk)

    lam_rows = lam_ref[...]
    lam = (jnp.exp(jnp.sum(lam_rows[0:1] * lam_rows[1:2], axis=-1, keepdims=True))
           - jnp.exp(jnp.sum(lam_rows[2:3] * lam_rows[3:4], axis=-1, keepdims=True)) + lambda_init)
    o_t = acc_ref[0] / l_ref[0] - lam * (acc_ref[1] / l_ref[1])
    o = o_t.T
    o = o * lax.rsqrt(jnp.mean(o * o, axis=-1, keepdims=True) + LN_EPS) * g_ref[...]
    o_ref[...] = (o * (1.0 - lambda_init)).astype(o_ref.dtype)


def _diff_attention(proj, lam_rows, subln_g, *, lambda_init, tq, tk):
    b, s, n3 = proj.shape
    d_model = n3 // 3
    heads = d_model // DA_V_DIM
    kern = functools.partial(_da_kernel, tq=tq, tk=tk, lambda_init=lambda_init)
    return pl.pallas_call(
        kern,
        grid=(b, heads, s // tq),
        in_specs=[
            pl.BlockSpec((None, tq, DA_V_DIM), lambda bi, h, i: (bi, i, h)),
            pl.BlockSpec((None, s, DA_V_DIM), lambda bi, h, i: (bi, 0, heads + h)),
            pl.BlockSpec((None, s, DA_V_DIM), lambda bi, h, i: (bi, 0, 2 * heads + h)),
            pl.BlockSpec((4, HEAD_DIM), lambda bi, h, i: (0, 0)),
            pl.BlockSpec((1, DA_V_DIM), lambda bi, h, i: (0, 0)),
        ],
        out_specs=pl.BlockSpec((None, tq, DA_V_DIM), lambda bi, h, i: (bi, i, h)),
        out_shape=jax.ShapeDtypeStruct((b, s, d_model), BF16),
        scratch_shapes=[pltpu.VMEM((s // tk, DA_V_DIM, tk), BF16), pltpu.VMEM((2, DA_V_DIM, tq), F32),
                        pltpu.VMEM((2, 1, tq), F32), pltpu.VMEM((2, 1, tq), F32)],
        compiler_params=_cparams(("parallel", "parallel", "arbitrary")),
        name="diff_attention",
    )(proj, proj, proj, lam_rows, subln_g)


def _sb_kernel(q_ref, k_ref, v_ref, o_ref, vt_ref, acc_ref, run_ref, *, tq, tk, hp):
    i = pl.program_id(2)
    diag_tiles = tq // tk
    heads = range(hp)

    def head(x, h, axis):
        return x[:, h * HEAD_DIM:(h + 1) * HEAD_DIM] if axis == 1 else x[h * HEAD_DIM:(h + 1) * HEAD_DIM, :]

    @pl.when(i == 0)
    def _():
        _store_v_transposed(v_ref, vt_ref, tk)

    acc_ref[...] = jnp.zeros(acc_ref.shape, F32)
    run_ref[...] = jnp.zeros(run_ref.shape, F32)
    q_t = _transpose_bf16(q_ref[...])
    key = lax.broadcasted_iota(jnp.int32, (tk, tq), 0)
    qry = lax.broadcasted_iota(jnp.int32, (tk, tq), 1)
    k_a = lax.broadcasted_iota(jnp.int32, (tk, tk), 0)
    k_b = lax.broadcasted_iota(jnp.int32, (tk, tk), 1)
    later = (k_b > k_a).astype(BF16)

    def step(j, shift):
        off = pl.multiple_of(j * tk, tk)
        ks = k_ref[pl.ds(off, tk), :]
        v_t = vt_ref[j]
        z = [jnp.dot(head(ks, h, 1), head(q_t, h, 0), preferred_element_type=F32) for h in heads]
        log_beta = [jnp.minimum(zh, 0.0) - jnp.log2(1.0 + jnp.exp2(-jnp.abs(zh))) for zh in z]
        log_1m = [log_beta[h] - z[h] for h in heads]
        if shift is not None:
            strict = key + shift < qry
            log_1m = [jnp.where(strict, x, 0.0) for x in log_1m]
        after = [jnp.dot(later, x.astype(BF16), preferred_element_type=F32) for x in log_1m]
        run = [run_ref[h] for h in heads]
        a = [jnp.exp2(log_beta[h] + after[h] + run[h]) for h in heads]
        if shift is not None:
            a = [jnp.where(strict, x, 0.0) for x in a]
        for h in heads:
            acc_ref[h] += jnp.dot(head(v_t, h, 0), a[h].astype(BF16), preferred_element_type=F32)
            run_ref[h] = run[h] + after[h][0:1, :] + log_1m[h][0:1, :]

    for dj in reversed(range(diag_tiles)):
        step(i * diag_tiles + dj, dj * tk)

    def body(jj, carry):
        step(i * diag_tiles - 1 - jj, None)
        return carry

    lax.fori_loop(0, i * diag_tiles, body, 0)
    for h in heads:
        o_ref[:, h * HEAD_DIM:(h + 1) * HEAD_DIM] = acc_ref[h].T.astype(o_ref.dtype)


def _stick_breaking_attention(proj, *, tq, tk, hp):
    b, s, n3 = proj.shape
    d_model = n3 // 3
    heads = d_model // HEAD_DIM
    groups = heads // hp
    width = hp * HEAD_DIM
    kern = functools.partial(_sb_kernel, tq=tq, tk=tk, hp=hp)
    return pl.pallas_call(
        kern,
        grid=(b, groups, s // tq),
        in_specs=[
            pl.BlockSpec((None, tq, width), lambda bi, h, i: (bi, i, h)),
            pl.BlockSpec((None, s, width), lambda bi, h, i: (bi, 0, groups + h)),
            pl.BlockSpec((None, s, width), lambda bi, h, i: (bi, 0, 2 * groups + h)),
        ],
        out_specs=pl.BlockSpec((None, tq, width), lambda bi, h, i: (bi, i, h)),
        out_shape=jax.ShapeDtypeStruct((b, s, d_model), BF16),
        scratch_shapes=[pltpu.VMEM((s // tk, width, tk), BF16), pltpu.VMEM((hp, HEAD_DIM, tq), F32),
                        pltpu.VMEM((hp, 1, tq), F32)],
        compiler_params=_cparams(("parallel", "parallel", "arbitrary")),
        name="stick_breaking_attention",
    )(proj, proj, proj)


def _outproj_ln_kernel(o_ref, w_ref, h_ref, g_ref, b_ref, out_ref, *, alpha):
    mix = jnp.dot(o_ref[...], w_ref[...], preferred_element_type=F32)
    out_ref[...] = _layer_norm_rows(alpha * h_ref[...] + mix, g_ref[...], b_ref[...])


def _out_projection_ln(o2d, w_bf16, h2d, g, b, *, alpha, tm):
    m, d = h2d.shape
    kern = functools.partial(_outproj_ln_kernel, alpha=alpha)
    return pl.pallas_call(
        kern,
        grid=(m // tm,),
        in_specs=[
            pl.BlockSpec((tm, d), lambda i: (i, 0)),
            pl.BlockSpec((d, d), lambda i: (0, 0)),
            pl.BlockSpec((tm, d), lambda i: (i, 0)),
            pl.BlockSpec((1, d), lambda i: (0, 0)),
            pl.BlockSpec((1, d), lambda i: (0, 0)),
        ],
        out_specs=pl.BlockSpec((tm, d), lambda i: (i, 0)),
        out_shape=jax.ShapeDtypeStruct((m, d), F32),
        compiler_params=_cparams(("parallel",)),
        name="out_projection_ln",
    )(o2d, w_bf16, h2d, g, b)


def _router_kernel(h_ref, w_ref, b_ref, idx_ref, gate_ref, rank_ref, cnt_ref, carry_ref, *, n_experts):
    step = pl.program_id(0)

    @pl.when(step == 0)
    def _():
        carry_ref[...] = jnp.zeros(carry_ref.shape, F32)

    tm = h_ref.shape[0]
    h = h_ref[...]
    h_hi = h.astype(BF16)
    h_lo = (h - h_hi.astype(F32)).astype(BF16)
    w_hi = w_ref[0]
    logits = (jnp.dot(h_hi, w_hi, preferred_element_type=F32)
              + jnp.dot(h_hi, w_ref[1], preferred_element_type=F32)
              + jnp.dot(h_lo, w_hi, preferred_element_type=F32)) + b_ref[...]
    lane = lax.broadcasted_iota(jnp.int32, (tm, LANES), 1)
    work = jnp.where(lane < n_experts, logits, -jnp.inf)

    vals, idxs = [], []
    onehot = jnp.zeros((tm, LANES), F32)
    for _ in range(TOP_K):
        mx = jnp.max(work, axis=-1, keepdims=True)
        ix = jnp.min(jnp.where(work == mx, lane, LANES), axis=-1, keepdims=True)
        sel = lane == ix
        vals.append(mx)
        idxs.append(ix)
        onehot = jnp.where(sel, 1.0, onehot)
        work = jnp.where(sel, -jnp.inf, work)

    exps = [jnp.exp(v - vals[0]) for v in vals]
    denom = exps[0]
    for e in exps[1:]:
        denom = denom + e

    r_i = lax.broadcasted_iota(jnp.int32, (tm, tm), 0)
    c_i = lax.broadcasted_iota(jnp.int32, (tm, tm), 1)
    lower = (c_i < r_i).astype(BF16)
    earlier = jnp.dot(lower, onehot.astype(BF16), preferred_element_type=F32) + carry_ref[...]

    idx_out = jnp.zeros((tm, LANES), jnp.int32)
    gate_out = jnp.zeros((tm, LANES), F32)
    rank_out = jnp.zeros((tm, LANES), jnp.int32)
    for k in range(TOP_K):
        rk = jnp.sum(jnp.where(lane == idxs[k], earlier, 0.0), axis=-1, keepdims=True)
        idx_out = jnp.where(lane == k, idxs[k], idx_out)
        gate_out = jnp.where(lane == k, exps[k] / denom, gate_out)
        rank_out = jnp.where(lane == k, rk.astype(jnp.int32), rank_out)
    idx_ref[...] = idx_out
    gate_ref[...] = gate_out
    rank_ref[...] = rank_out

    total = carry_ref[...] + jnp.sum(onehot, axis=0, keepdims=True)
    carry_ref[...] = total
    cnt_ref[...] = total.astype(jnp.int32)


def _router(h2d, w_pad, b_pad, *, n_experts, tm):
    t, d = h2d.shape
    kern = functools.partial(_router_kernel, n_experts=n_experts)
    tok_spec = pl.BlockSpec((tm, LANES), lambda i: (i, 0))
    return pl.pallas_call(
        kern,
        grid=(t // tm,),
        in_specs=[
            pl.BlockSpec((tm, d), lambda i: (i, 0)),
            pl.BlockSpec((2, d, LANES), lambda i: (0, 0, 0)),
            pl.BlockSpec((1, LANES), lambda i: (0, 0)),
        ],
        out_specs=[tok_spec, tok_spec, tok_spec, pl.BlockSpec((1, LANES), lambda i: (0, 0))],
        out_shape=[jax.ShapeDtypeStruct((t, LANES), jnp.int32), jax.ShapeDtypeStruct((t, LANES), F32),
                   jax.ShapeDtypeStruct((t, LANES), jnp.int32), jax.ShapeDtypeStruct((1, LANES), jnp.int32)],
        scratch_shapes=[pltpu.VMEM((1, LANES), F32)],
        compiler_params=_cparams(("arbitrary",)),
        name="router",
    )(h2d, w_pad, b_pad)


def _dispatch_kernel(dest_ref, pad_ref, h_ref, x_hbm, zero_ref, sem, *, tm, tail_start, tail_rows):
    step = pl.program_id(0)
    base = step * tm

    @pl.when(step == 0)
    def _():
        zero_ref[...] = jnp.zeros(zero_ref.shape, F32)

        def tail_copy(n):
            return pltpu.make_async_copy(zero_ref, x_hbm.at[pl.ds(tail_start + n * ZERO_ROWS, ZERO_ROWS), :], sem)

        for n in range(tail_rows // ZERO_ROWS):
            tail_copy(n).start()
        for n in range(tail_rows // ZERO_ROWS):
            tail_copy(n).wait()

        def pad_copy(i):
            return pltpu.make_async_copy(h_ref.at[pl.ds(0, 1), :], x_hbm.at[pl.ds(pad_ref[i], 1), :], sem)

        def pad_start(i, carry):
            @pl.when(pad_ref[i] >= 0)
            def _():
                pad_copy(i).start()
            return carry

        def pad_wait(i, carry):
            @pl.when(pad_ref[i] >= 0)
            def _():
                pad_copy(i).wait()
            return carry

        lax.fori_loop(0, pad_ref.shape[0], pad_start, 0)
        lax.fori_loop(0, pad_ref.shape[0], pad_wait, 0)

    def row_copy(t, k):
        r = dest_ref[(base + t) * TOP_K + k]
        return pltpu.make_async_copy(h_ref.at[pl.ds(t, 1), :], x_hbm.at[pl.ds(r, 1), :], sem)

    def issue(t, carry):
        for k in range(TOP_K):
            row_copy(t, k).start(priority=k % 2)
        return carry

    lax.fori_loop(0, tm, issue, 0, unroll=DMA_ISSUE_UNROLL)
    for _ in range(TOP_K):
        pltpu.make_async_copy(h_ref, x_hbm.at[pl.ds(0, tm), :], sem).wait()


def _dispatch(dest_flat, pad_rows, h2d, *, p_rows, tm):
    t, d = h2d.shape
    tail_start = t * TOP_K
    tail_rows = p_rows - tail_start
    assert tail_rows % ZERO_ROWS == 0
    kern = functools.partial(_dispatch_kernel, tm=tm, tail_start=tail_start, tail_rows=tail_rows)
    return pl.pallas_call(
        kern,
        grid_spec=pltpu.PrefetchScalarGridSpec(
            num_scalar_prefetch=2,
            grid=(t // tm,),
            in_specs=[pl.BlockSpec((tm, d), lambda i, dest, pad: (i, 0))],
            out_specs=pl.BlockSpec(memory_space=pl.ANY),
            scratch_shapes=[pltpu.VMEM((ZERO_ROWS, d), F32), pltpu.SemaphoreType.DMA(())],
        ),
        out_shape=jax.ShapeDtypeStruct((p_rows, d), F32),
        compiler_params=_cparams(("arbitrary",)),
        name="dispatch",
    )(dest_flat, pad_rows, h2d)


def _expert_kernel(ce_ref, cr_ref, cn_ref, x_hbm, w1g_ref, w1l_ref, b1g_ref, b1l_ref, w2_ref, b2_ref, y_hbm,
                   xb_ref, y_ref, stage_ref, w1g_b, w1l_b, w2_b, sem, *, nf, n_chunks):
    del ce_ref
    s = pl.program_id(0)
    item = jnp.maximum(s - 1, 0)
    c = item // nf
    f = item % nf
    nv = jnp.where(s > 0, cn_ref[c], 0)
    d = y_ref.shape[2]
    yslot = c % 2
    cur = (s + 1) % 2
    nxt = s % 2

    def cast_next_weights():
        w1g_b[nxt] = w1g_ref[...].astype(BF16)
        w1l_b[nxt] = w1l_ref[...].astype(BF16)
        w2_b[nxt] = w2_ref[...].astype(BF16)

    @pl.when(s == 0)
    def _():
        cast_next_weights()

    def x_copy(chunk):
        start = pl.multiple_of(cr_ref[chunk], SUBLANES)
        return pltpu.make_async_copy(x_hbm.at[pl.ds(start, CHUNK_ROWS), :], stage_ref, sem.at[0])

    @pl.when((f == 0) & (nv > 0) & (c == 0))
    def _():
        x_copy(c).start()

    @pl.when((f == 0) & (nv > 0))
    def _():
        x_copy(c).wait()
        xb_ref[...] = stage_ref[...].astype(BF16)
        y_ref[yslot] = jnp.broadcast_to(b2_ref[...], y_ref.shape[1:])

    c_next = jnp.minimum(c + 1, n_chunks - 1)

    @pl.when((f == 1) & (nv > 0) & (c + 1 < n_chunks) & (cn_ref[c_next] > 0))
    def _():
        x_copy(c_next).start()

    def compute(rows):
        cast_next_weights()
        blk = PIPE_ROWS if rows % PIPE_ROWS == 0 else rows
        col = min(512, d)

        def first(r0):
            xs = xb_ref[r0:r0 + blk, :]
            hg = jnp.dot(xs, w1g_b[cur], preferred_element_type=F32) + b1g_ref[...]
            hl = jnp.dot(xs, w1l_b[cur], preferred_element_type=F32) + b1l_ref[...]
            return hg, hl

        def activate(hg, hl):
            glu = jnp.minimum(hg, SWIGLU_LIMIT)
            lin = jnp.clip(hl, -SWIGLU_LIMIT, SWIGLU_LIMIT)
            return (glu * jax.nn.sigmoid(SWIGLU_ALPHA * glu) * (lin + 1.0)).astype(BF16)

        def second(r0, act):
            for n in range(d // col):
                y_ref[yslot, r0:r0 + blk, n * col:(n + 1) * col] += jnp.dot(
                    act, w2_b[cur, :, n * col:(n + 1) * col], preferred_element_type=F32)

        starts = list(range(0, rows, blk))
        h = first(starts[0])
        for k, r0 in enumerate(starts):
            act = activate(*h)
            if k + 1 < len(starts):
                h = first(starts[k + 1])
            second(r0, act)

    @pl.when(nv > SMALL_ROWS)
    def _():
        compute(CHUNK_ROWS)

    @pl.when((nv > 0) & (nv <= SMALL_ROWS))
    def _():
        compute(SMALL_ROWS)

    sizes = []
    size = SUBLANES
    while size <= CHUNK_ROWS:
        sizes.append(size)
        size *= 2

    def writeback(chunk, action):
        rows_valid = cn_ref[chunk]
        start = pl.multiple_of(cr_ref[chunk], SUBLANES)
        slot = chunk % 2
        for size in sizes:
            @pl.when((rows_valid & size) != 0)
            def _():
                off = pl.multiple_of(rows_valid - (rows_valid % (2 * size)), SUBLANES)
                cp = pltpu.make_async_copy(y_ref.at[slot, pl.ds(off, size), :],
                                           y_hbm.at[pl.ds(start + off, size), :], sem.at[1 + slot])
                if action == "start":
                    cp.start()
                else:
                    cp.wait()

    last_step_of_chunk = (f == nf - 1) & (nv > 0)

    @pl.when(last_step_of_chunk & (c > 0))
    def _():
        writeback(jnp.maximum(c - 1, 0), "wait")

    @pl.when(last_step_of_chunk)
    def _():
        writeback(c, "start")

    @pl.when(last_step_of_chunk & ((c + 1 >= n_chunks) | (cn_ref[c_next] == 0)))
    def _():
        writeback(c, "wait")


def _expert_ffn(chunk_e, chunk_row, chunk_nv, n_steps, x_buf, w1, b1, w2, b2, *, layer, tf):
    p, d = x_buf.shape
    n_experts, _, f2 = w1.shape[1:]
    d_ff = f2 // 2
    nf = d_ff // tf
    nc = chunk_e.shape[0]
    b1r = b1.reshape(b1.shape[0], n_experts, 1, f2)
    b2r = b2.reshape(b2.shape[0], n_experts, 1, d)

    n_items = nc * nf

    def expert_and_tile(item, ce, cn):
        c = item // nf
        return ce[c], jnp.where(cn[c] > 0, item % nf, nf - 1)

    def weights_of_next(s, ce, cn):
        return expert_and_tile(jnp.minimum(s, n_items - 1), ce, cn)

    def biases_of_current(s, ce, cn):
        return expert_and_tile(jnp.maximum(s - 1, 0), ce, cn)

    def w1_map(half):
        def index(s, ce, cr, cn):
            e, f = weights_of_next(s, ce, cn)
            return layer, e, 0, half * nf + f
        return index

    def b1_map(half):
        def index(s, ce, cr, cn):
            e, f = biases_of_current(s, ce, cn)
            return layer, e, 0, half * nf + f
        return index

    def w2_map(s, ce, cr, cn):
        e, f = weights_of_next(s, ce, cn)
        return layer, e, f, 0

    def b2_map(s, ce, cr, cn):
        e, _ = biases_of_current(s, ce, cn)
        return layer, e, 0, 0

    assert nf >= 2, "the next chunk's rows are requested during a chunk's second d_ff tile"
    kern = functools.partial(_expert_kernel, nf=nf, n_chunks=nc)
    return pl.pallas_call(
        kern,
        grid_spec=pltpu.PrefetchScalarGridSpec(
            num_scalar_prefetch=3,
            grid=(n_steps,),
            in_specs=[
                pl.BlockSpec(memory_space=pl.ANY),
                pl.BlockSpec((None, None, d, tf), w1_map(0)),
                pl.BlockSpec((None, None, d, tf), w1_map(1)),
                pl.BlockSpec((None, None, 1, tf), b1_map(0)),
                pl.BlockSpec((None, None, 1, tf), b1_map(1)),
                pl.BlockSpec((None, None, tf, d), w2_map),
                pl.BlockSpec((None, None, 1, d), b2_map),
            ],
            out_specs=pl.BlockSpec(memory_space=pl.ANY),
            scratch_shapes=[
                pltpu.VMEM((CHUNK_ROWS, d), BF16),
                pltpu.VMEM((2, CHUNK_ROWS, d), F32),
                pltpu.VMEM((CHUNK_ROWS, d), F32),
                pltpu.VMEM((2, d, tf), BF16),
                pltpu.VMEM((2, d, tf), BF16),
                pltpu.VMEM((2, tf, d), BF16),
                pltpu.SemaphoreType.DMA((3,)),
            ],
        ),
        out_shape=jax.ShapeDtypeStruct((p, d), F32),
        input_output_aliases={3: 0},
        compiler_params=_cparams(("arbitrary",)),
        name="expert_ffn",
    )(chunk_e, chunk_row, chunk_nv, x_buf, w1, w1, b1r, b1r, w2, b2r)


def _combine_kernel(dest_ref, y_hbm, gate_ref, h_ref, g_ref, b_ref, out_ref, buf_ref, sem, *, tm, alpha):
    step = pl.program_id(0)
    slot = step % 2

    def gather(tile, into):
        def issue(t, carry):
            for k in range(TOP_K):
                r = dest_ref[(tile * tm + t) * TOP_K + k]
                pltpu.make_async_copy(y_hbm.at[pl.ds(r, 1), :], buf_ref.at[into, k, pl.ds(t, 1), :],
                                      sem.at[into]).start(priority=k % 2)
            return carry
        lax.fori_loop(0, tm, issue, 0, unroll=DMA_ISSUE_UNROLL)

    @pl.when(step == 0)
    def _():
        gather(0, 0)

    @pl.when(step + 1 < pl.num_programs(0))
    def _():
        gather(step + 1, 1 - slot)

    for k in range(TOP_K):
        pltpu.make_async_copy(y_hbm.at[pl.ds(0, tm), :], buf_ref.at[slot, k], sem.at[slot]).wait()

    gates = gate_ref[...]
    y = gates[:, 0:1] * buf_ref[slot, 0]
    for k in range(1, TOP_K):
        y = y + gates[:, k:k + 1] * buf_ref[slot, k]
    out_ref[...] = _layer_norm_rows(alpha * h_ref[...] + y, g_ref[...], b_ref[...])


def _combine_ln(dest_flat, y_buf, gates, h2d, g, b, *, alpha, tm):
    t, d = h2d.shape
    kern = functools.partial(_combine_kernel, tm=tm, alpha=alpha)
    return pl.pallas_call(
        kern,
        grid_spec=pltpu.PrefetchScalarGridSpec(
            num_scalar_prefetch=1,
            grid=(t // tm,),
            in_specs=[
                pl.BlockSpec(memory_space=pl.ANY),
                pl.BlockSpec((tm, LANES), lambda i, dest: (i, 0)),
                pl.BlockSpec((tm, d), lambda i, dest: (i, 0)),
                pl.BlockSpec((1, d), lambda i, dest: (0, 0)),
                pl.BlockSpec((1, d), lambda i, dest: (0, 0)),
            ],
            out_specs=pl.BlockSpec((tm, d), lambda i, dest: (i, 0)),
            scratch_shapes=[pltpu.VMEM((2, TOP_K, tm, d), F32), pltpu.SemaphoreType.DMA((2,))],
        ),
        out_shape=jax.ShapeDtypeStruct((t, d), F32),
        compiler_params=_cparams(("arbitrary",)),
        name="combine_ln",
    )(dest_flat, y_buf, gates, h2d, g, b)


def _routing_tables(idx, rank, counts, *, n_chunks):
    padded = (counts + SUBLANES - 1) // SUBLANES * SUBLANES
    pad_end = jnp.cumsum(padded)
    pad_start = pad_end - padded
    experts = jnp.arange(counts.shape[0], dtype=idx.dtype)
    dest = jnp.sum(jnp.where(idx[..., None] == experts, pad_start, 0), axis=-1) + rank
    nchunk = (padded + CHUNK_ROWS - 1) // CHUNK_ROWS
    chunk_end = jnp.cumsum(nchunk)
    chunk_start = chunk_end - nchunk
    total = chunk_end[-1]
    cids = jnp.arange(n_chunks, dtype=jnp.int32)
    clamped = jnp.minimum(cids, total - 1)
    ce = jnp.sum((chunk_end[None, :] <= clamped[:, None]).astype(jnp.int32), axis=1)
    local = clamped - chunk_start[ce]
    crow = (pad_start[ce] + local * CHUNK_ROWS).astype(jnp.int32)
    cnv = jnp.clip(padded[ce] - local * CHUNK_ROWS, 0, CHUNK_ROWS)
    cnv = jnp.where(cids < total, cnv, 0).astype(jnp.int32)
    fill = counts[:, None] + jnp.arange(SUBLANES, dtype=jnp.int32)[None, :]
    pad_rows = jnp.where(fill < padded[:, None], pad_start[:, None] + fill, -1)
    return (dest.reshape(-1).astype(jnp.int32), pad_rows.reshape(-1).astype(jnp.int32),
            ce.astype(jnp.int32), crow, cnv)


def _moe_ln(h2d, router_w, router_b, w1, b1, w2, b2, g, b, *, layer, alpha, tiles):
    t, d = h2d.shape
    n_experts = router_w.shape[1]
    w_pad = jnp.pad(router_w, ((0, 0), (0, LANES - n_experts)))
    b_pad = jnp.pad(router_b, (0, LANES - n_experts)).reshape(1, LANES)
    w_hi = w_pad.astype(BF16)
    w_parts = jnp.stack([w_hi, (w_pad - w_hi.astype(F32)).astype(BF16)])
    idx, gates, rank, counts = _router(h2d, w_parts, b_pad, n_experts=n_experts, tm=tiles["router_tm"])
    n_chunks = (t * TOP_K + n_experts * SUBLANES) // CHUNK_ROWS + n_experts
    dest_flat, pad_rows, ce, crow, cnv = _routing_tables(idx[:, :TOP_K], rank[:, :TOP_K],
                                                         counts[0, :n_experts], n_chunks=n_chunks)
    p_rows = t * TOP_K + n_experts * SUBLANES + CHUNK_ROWS
    x_buf = _dispatch(dest_flat, pad_rows, h2d, p_rows=p_rows, tm=tiles["dispatch_tm"])
    used_chunks = jnp.sum((cnv > 0).astype(jnp.int32))
    n_steps = used_chunks * (w2.shape[2] // tiles["expert_tf"]) + 1
    y_buf = _expert_ffn(ce, crow, cnv, n_steps, x_buf, w1, b1, w2, b2, layer=layer, tf=tiles["expert_tf"])
    return _combine_ln(dest_flat, y_buf, gates, h2d, g, b, alpha=alpha, tm=tiles["combine_tm"])


def _rope_tables(positions, q_scale):
    inv_freq = ROPE_THETA ** (-jnp.arange(0, HEAD_DIM, 2, dtype=F32) / HEAD_DIM)
    ang = positions.astype(F32).reshape(-1)[:, None] * inv_freq
    cos, sin = jnp.cos(ang), jnp.sin(ang)
    cos2 = jnp.concatenate([cos, cos], axis=-1)
    sin2 = jnp.concatenate([-sin, sin], axis=-1)
    k_tab = jnp.stack([cos2, sin2])
    return jnp.stack([k_tab * q_scale, k_tab])


def _tiles(t, s, d, d_ff):
    return {
        "inproj_tm": min(1024, t), "inproj_tn": min(1024, d),
        "da_tq": min(512, s), "da_tk": min(512, s),
        "sb_tq": min(512, s), "sb_tk": min(256, s), "sb_heads_per_step": min(4, d // HEAD_DIM),
        "outproj_tm": min(512, t),
        "router_tm": min(256, t),
        "dispatch_tm": min(512, t),
        "combine_tm": min(256, t),
        "expert_tf": min(256, d_ff),
    }


def kernel(x, positions, da_w_in, da_lambda_q1, da_lambda_k1, da_lambda_q2, da_lambda_k2, da_subln_g, da_w_out,
           sb_w_in, sb_w_out, ln1_g, ln1_b, ln2_g, ln2_b, router_w, router_b, expert_w1, expert_b1, expert_w2,
           expert_b2):
    bsz, seq, d = x.shape
    t = bsz * seq
    depth = ln1_g.shape[0]
    d_ff = expert_w2.shape[2]
    alpha = (2 * depth) ** 0.25
    q_scale = HEAD_DIM ** -0.5 * math.log2(math.e)
    tiles = _tiles(t, seq, d, d_ff)
    tab = _rope_tables(positions, q_scale)

    h = x.reshape(t, d)
    for i in range(depth):
        j = i // 2
        if i % 2 == 0:
            lambda_init = 0.8 - 0.6 * math.exp(-0.3 * i)
            proj = _in_projection(h, da_w_in[j].astype(BF16), tab, rope=True, d_model=d, q_scale=q_scale,
                                  tm=tiles["inproj_tm"], tn=tiles["inproj_tn"])
            lam_rows = jnp.stack([da_lambda_q1[j], da_lambda_k1[j], da_lambda_q2[j], da_lambda_k2[j]])
            o = _diff_attention(proj.reshape(bsz, seq, 3 * d), lam_rows, da_subln_g[j].reshape(1, DA_V_DIM),
                                lambda_init=lambda_init, tq=tiles["da_tq"], tk=tiles["da_tk"])
            w_out = da_w_out[j]
        else:
            proj = _in_projection(h, sb_w_in[j].astype(BF16), tab, rope=False, d_model=d, q_scale=q_scale,
                                  tm=tiles["inproj_tm"], tn=tiles["inproj_tn"])
            o = _stick_breaking_attention(proj.reshape(bsz, seq, 3 * d), tq=tiles["sb_tq"], tk=tiles["sb_tk"],
                                          hp=tiles["sb_heads_per_step"])
            w_out = sb_w_out[j]
        h = _out_projection_ln(o.reshape(t, d), w_out.astype(BF16), h, ln1_g[i].reshape(1, d),
                               ln1_b[i].reshape(1, d), alpha=alpha, tm=tiles["outproj_tm"])
        h = _moe_ln(h, router_w[i], router_b[i], expert_w1, expert_b1, expert_w2, expert_b2,
                    ln2_g[i].reshape(1, d), ln2_b[i].reshape(1, d), layer=i, alpha=alpha, tiles=tiles)
    return h.reshape(bsz, seq, d)
```
